```python
import math
import jax, jax.numpy as jnp
from jax import lax
import numpy as np

D_MODEL = 1024
BATCH = 8
SEQ = 2048
DEPTH = 2
DEC_BATCH = 16
DEC_SEQ = 64
PAST_LEN = 1024

CHUNK = 64
N_MIXERS = 2
N_POOL_LAYERS = (DEPTH + 1) // 2
N_SSM_LAYERS = DEPTH // 2
POOL_WINDOWS = (2, 4, 8, 16)
POOL_GROUP = D_MODEL // len(POOL_WINDOWS)
POOL_HIST = max(POOL_WINDOWS) - 1
SSM_GROUP_CH = 16
SSM_GROUPS = D_MODEL // SSM_GROUP_CH
SSM_STATE = 64
N_EXPERTS = 32
TOP_K = 4
D_FF = D_MODEL
SWIGLU_LIMIT = 7.0
SWIGLU_ALPHA = 1.702
MOE_BLOCK = 256
RMS_EPS = 1e-5
DT_MIN = 0.001
DT_MAX = 0.1

kernel_name = 'hybrid_pool_s5_moe_stream_step'

F32 = jnp.float32


def rms_norm(x, g):
    xf = x.astype(F32)
    y = xf * lax.rsqrt(jnp.mean(xf * xf, axis=-1, keepdims=True) + RMS_EPS) * g.astype(F32)
    return y.astype(x.dtype)


def pool_mixer(u, hist, start_pos, w_groups, scale):
    b, L, _ = u.shape
    uf = u.astype(F32)
    ext = jnp.concatenate([hist.astype(F32), uf], axis=1)
    cs = jnp.concatenate([jnp.zeros((b, 1, D_MODEL), F32), jnp.cumsum(ext, axis=1)], axis=1)
    pos = start_pos + jnp.arange(L, dtype=jnp.int32)
    outs = []
    for g, w in enumerate(POOL_WINDOWS):
        sl = slice(g * POOL_GROUP, (g + 1) * POOL_GROUP)
        upper = cs[:, POOL_HIST + 1:POOL_HIST + 1 + L, sl]
        lower = cs[:, POOL_HIST + 1 - w:POOL_HIST + 1 - w + L, sl]
        cnt = jnp.minimum(w, pos + 1).astype(F32)[None, :, None]
        mixed = (upper - lower) / cnt - uf[..., sl]
        outs.append(jnp.einsum('blc,cd->bld', mixed, w_groups[g].astype(F32)))
    y = jnp.concatenate(outs, axis=-1) * scale.astype(F32)
    return y.astype(u.dtype), ext[:, -POOL_HIST:].astype(u.dtype)


def _ssm_combine(e1, e2):
    a1, b1 = e1
    a2, b2 = e2
    return (a1 * a2, a2 * b1 + b2)


def s5_mixer(u, h0_re, h0_im, a_re, a_im, log_dt, b_re, b_im, c_re, c_im, d_skip, w_glu, w_gate):
    b, L, _ = u.shape
    uf = u.astype(F32)
    ug = uf.reshape(b, L, SSM_GROUPS, SSM_GROUP_CH)
    A = lax.complex(a_re.astype(F32), a_im.astype(F32))
    dt = jnp.exp(log_dt.astype(F32))[:, None]
    a_bar = jnp.exp(A * dt)
    Bc = lax.complex(b_re.astype(F32), b_im.astype(F32))
    Cc = lax.complex(c_re.astype(F32), c_im.astype(F32))
    b_bar = ((a_bar - 1.0) / A)[..., None] * Bc
    bu = jnp.einsum('blgh,gph->blgp', ug.astype(jnp.complex64), b_bar)
    h0 = lax.complex(h0_re.astype(F32), h0_im.astype(F32))
    bu = bu.at[:, 0].add(a_bar * h0)
    a_seq = jnp.broadcast_to(a_bar, bu.shape)
    _, h = lax.associative_scan(_ssm_combine, (a_seq, bu), axis=1)
    y = jnp.real(jnp.einsum('blgp,ghp->blgh', h, Cc)).reshape(b, L, D_MODEL) + d_skip.astype(F32) * uf
    g = jax.nn.gelu(y)
    out = jnp.einsum('bld,de->ble', g, w_glu.astype(F32)) * jax.nn.sigmoid(jnp.einsum('bld,de->ble', g, w_gate.astype(F32)))
    h_last = h[:, -1]
    return out.astype(u.dtype), jnp.real(h_last), jnp.imag(h_last)


def moe_ffn(u, w_r, b_r, w_up, b_up, w_dn, b_dn):
    shp = u.shape
    t = u.reshape(-1, D_MODEL)
    n = t.shape[0]
    logits = t.astype(F32) @ w_r.astype(F32) + b_r.astype(F32)
    top_v, top_i = lax.top_k(logits, TOP_K)
    top_w = jax.nn.softmax(top_v, axis=-1)
    gates = jnp.einsum('tk,tke->te', top_w, jax.nn.one_hot(top_i, N_EXPERTS, dtype=F32)).astype(t.dtype)
    pad = (-n) % MOE_BLOCK
    t = jnp.pad(t, ((0, pad), (0, 0)))
    gates = jnp.pad(gates, ((0, pad), (0, 0)))
    nb = (n + pad) // MOE_BLOCK

    def block(args):
        xb, gb = args
        gu = jnp.einsum('td,edf->tef', xb, w_up) + b_up
        glu = jnp.minimum(gu[..., :D_FF], SWIGLU_LIMIT)
        lin = jnp.clip(gu[..., D_FF:], -SWIGLU_LIMIT, SWIGLU_LIMIT)
        act = glu * jax.nn.sigmoid(SWIGLU_ALPHA * glu) * (lin + 1.0) * gb[..., None]
        return jnp.einsum('tef,efd->td', act, w_dn) + gb @ b_dn

    out = lax.map(block, (t.reshape(nb, MOE_BLOCK, D_MODEL), gates.reshape(nb, MOE_BLOCK, N_EXPERTS)))
    return out.reshape(-1, D_MODEL)[:n].reshape(shp).astype(u.dtype)


def setup_inputs(seed: int = 0) -> dict:
    key = jax.random.key(seed)
    ks = jax.random.split(key, 32)

    def nrm(k, shape, s):
        return s * jax.random.normal(k, shape, F32)

    G, P, H, E = SSM_GROUPS, SSM_STATE, SSM_GROUP_CH, N_EXPERTS
    return {
        'x_prompt': nrm(ks[0], (BATCH, SEQ, D_MODEL), 1.0),
        'x_sample': nrm(ks[1], (DEC_BATCH, DEC_SEQ, D_MODEL), 1.0),
        'cache_pool': nrm(ks[2], (N_POOL_LAYERS, DEC_BATCH, POOL_HIST, D_MODEL), 1.0),
        'state_ssm_re': nrm(ks[3], (N_SSM_LAYERS, DEC_BATCH, G, P), 0.3),
        'state_ssm_im': nrm(ks[4], (N_SSM_LAYERS, DEC_BATCH, G, P), 0.3),
        'norm_mix_g': 1.0 + nrm(ks[5], (DEPTH, D_MODEL), 0.02),
        'norm_ffn_g': 1.0 + nrm(ks[6], (DEPTH, D_MODEL), 0.02),
        'norm_final_g': 1.0 + nrm(ks[7], (D_MODEL,), 0.02),
        'pool_w': nrm(ks[8], (N_POOL_LAYERS, len(POOL_WINDOWS), POOL_GROUP, POOL_GROUP), POOL_GROUP ** -0.5),
        'pool_scale': 1.0 + nrm(ks[9], (N_POOL_LAYERS, D_MODEL), 0.02),
        'ssm_a_re': -0.5 + nrm(ks[10], (N_SSM_LAYERS, G, P), 0.01),
        'ssm_a_im': math.pi * jnp.arange(P, dtype=F32) + nrm(ks[11], (N_SSM_LAYERS, G, P), 0.01),
        'ssm_log_dt': jax.random.uniform(ks[12], (N_SSM_LAYERS, G), F32, math.log(DT_MIN), math.log(DT_MAX)),
        'ssm_b_re': nrm(ks[13], (N_SSM_LAYERS, G, P, H), (2.0 * H) ** -0.5),
        'ssm_b_im': nrm(ks[14], (N_SSM_LAYERS, G, P, H), (2.0 * H) ** -0.5),
        'ssm_c_re': nrm(ks[15], (N_SSM_LAYERS, G, H, P), 0.5),
        'ssm_c_im': nrm(ks[16], (N_SSM_LAYERS, G, H, P), 0.5),
        'ssm_d': nrm(ks[17], (N_SSM_LAYERS, D_MODEL), 1.0),
        'ssm_glu_w': nrm(ks[18], (N_SSM_LAYERS, D_MODEL, D_MODEL), D_MODEL ** -0.5),
        'ssm_glu_gate': nrm(ks[19], (N_SSM_LAYERS, D_MODEL, D_MODEL), D_MODEL ** -0.5),
        'router_w': nrm(ks[20], (DEPTH, D_MODEL, E), D_MODEL ** -0.5),
        'router_b': nrm(ks[21], (DEPTH, E), 0.01),
        'moe_w_up': nrm(ks[22], (DEPTH, E, D_MODEL, 2 * D_FF), D_MODEL ** -0.5),
        'moe_b_up': nrm(ks[23], (DEPTH, E, 2 * D_FF), 0.01),
        'moe_w_down': nrm(ks[24], (DEPTH, E, D_FF, D_MODEL), D_FF ** -0.5),
        'moe_b_down': nrm(ks[25], (DEPTH, E, D_MODEL), 0.01),
    }


def reference(x_prompt, x_sample, cache_pool, state_ssm_re, state_ssm_im,
              norm_mix_g, norm_ffn_g, norm_final_g, pool_w, pool_scale,
              ssm_a_re, ssm_a_im, ssm_log_dt, ssm_b_re, ssm_b_im, ssm_c_re, ssm_c_im,
              ssm_d, ssm_glu_w, ssm_glu_gate,
              router_w, router_b, moe_w_up, moe_b_up, moe_w_down, moe_b_down):

    def trunk(x, pool_hist, ssm_re, ssm_im, start_pos):
        new_pool, new_re, new_im = [], [], []
        for i in range(DEPTH):
            u = rms_norm(x, norm_mix_g[i])
            j = i // N_MIXERS
            if i % N_MIXERS == 0:
                y, nh = pool_mixer(u, pool_hist[j], start_pos, pool_w[j], pool_scale[j])
                new_pool.append(nh)
            else:
                y, hr, hi = s5_mixer(u, ssm_re[j], ssm_im[j], ssm_a_re[j], ssm_a_im[j], ssm_log_dt[j],
                                     ssm_b_re[j], ssm_b_im[j], ssm_c_re[j], ssm_c_im[j], ssm_d[j],
                                     ssm_glu_w[j], ssm_glu_gate[j])
                new_re.append(hr)
                new_im.append(hi)
            x = x + y
            x = x + moe_ffn(rms_norm(x, norm_ffn_g[i]), router_w[i], router_b[i],
                            moe_w_up[i], moe_b_up[i], moe_w_down[i], moe_b_down[i])
        return rms_norm(x, norm_final_g), jnp.stack(new_pool), jnp.stack(new_re), jnp.stack(new_im)

    bp = x_prompt.shape[0]
    zero_pool = jnp.zeros((N_POOL_LAYERS, bp, POOL_HIST, D_MODEL), x_prompt.dtype)
    zero_ssm = jnp.zeros((N_SSM_LAYERS, bp, SSM_GROUPS, SSM_STATE), F32)
    y_prompt, pool_p, re_p, im_p = trunk(x_prompt, zero_pool, zero_ssm, zero_ssm, 0)
    y_sample, pool_s, re_s, im_s = trunk(x_sample, cache_pool, state_ssm_re, state_ssm_im, PAST_LEN)
    return (y_prompt, y_sample, pool_p, pool_s, re_p, im_p, re_s, im_s)
```

```python
import functools
import math

import jax
import jax.numpy as jnp
from jax import lax
from jax.experimental import pallas as pl
from jax.experimental.pallas import tpu as pltpu

F32 = jnp.float32
BF16 = jnp.bfloat16

POOL_WINDOWS = (2, 4, 8, 16)
POOL_HIST = max(POOL_WINDOWS) - 1
SSM_GROUP_CH = 16
SSM_STATE = 64
TOP_K = 4
SWIGLU_LIMIT = 7.0
SWIGLU_ALPHA = 1.702
RMS_EPS = 1e-5
PAST_LEN = 1024

SUBLANES = 8
LANES = 128
MXU_DIM = 256
VMEM_LIMIT = 56 * 1024 * 1024

MIX_ROWS = 512
TOKEN_BLOCK = 256
FFN_TILE = 256
SSM_SLAB = MXU_DIM


def _rms(x, g):
    return x * lax.rsqrt(jnp.mean(x * x, axis=-1, keepdims=True) + RMS_EPS) * g


def _params(**kw):
    return pltpu.CompilerParams(dimension_semantics=("arbitrary",), vmem_limit_bytes=VMEM_LIMIT, **kw)


def _pool_kernel(x_ref, hist_ref, g_ref, w_ref, scale_ref, alias_ref, o_ref, hist_out_ref, ext_ref,
                 *, bt, tc, start_pos):
    del alias_ref
    i = pl.program_id(0)
    rc = tc * bt
    hr = POOL_HIST * bt
    gw = x_ref.shape[1] // len(POOL_WINDOWS)

    @pl.when(i == 0)
    def _():
        ext_ref[0:hr, :] = hist_ref[...]

    x = x_ref[...]
    u = _rms(x, g_ref[...])
    ext_ref[hr:hr + rc, :] = u
    row = lax.broadcasted_iota(jnp.int32, (rc, 1), 0)
    pos = start_pos + i * tc + row // bt
    outs = []
    for gi, w in enumerate(POOL_WINDOWS):
        c0 = gi * gw
        s = ext_ref[hr:hr + rc, c0:c0 + gw]
        for j in range(1, w):
            s = s + ext_ref[hr - j * bt:hr - j * bt + rc, c0:c0 + gw]
        cnt = jnp.minimum(w, pos + 1).astype(F32)
        mixed = s / cnt - u[:, c0:c0 + gw]
        outs.append(jnp.dot(mixed.astype(BF16), w_ref[gi], preferred_element_type=F32))
    y = jnp.concatenate(outs, axis=-1) * scale_ref[...]
    o_ref[...] = x + y
    ext_ref[0:hr, :] = ext_ref[rc:rc + hr, :]
    hist_out_ref[...] = ext_ref[0:hr, :]


def _pool_call(x_all, prev_out, hist, g, w_bf, scale, *, row0, n_rows, bt, start_pos):
    n, d = x_all.shape
    rc = MIX_ROWS
    tc = rc // bt
    hr = POOL_HIST * bt
    assert n_rows % rc == 0 and row0 % rc == 0 and rc > hr
    blk0 = row0 // rc
    kern = functools.partial(_pool_kernel, bt=bt, tc=tc, start_pos=start_pos)
    const2 = lambda i: (0, 0)
    args = [x_all, hist, g, w_bf, scale]
    in_specs = [
        pl.BlockSpec((rc, d), lambda i: (i + blk0, 0)),
        pl.BlockSpec((hr, d), const2),
        pl.BlockSpec((1, d), const2),
        pl.BlockSpec(w_bf.shape, lambda i: (0, 0, 0)),
        pl.BlockSpec((1, d), const2),
    ]
    aliases = {}
    if prev_out is None:
        prev_out = jnp.zeros((SUBLANES, LANES), F32)
    else:
        aliases = {5: 0}
    args.append(prev_out)
    in_specs.append(pl.BlockSpec(memory_space=pl.ANY))
    return pl.pallas_call(
        kern,
        grid=(n_rows // rc,),
        in_specs=in_specs,
        out_specs=[pl.BlockSpec((rc, d), lambda i: (i + blk0, 0)), pl.BlockSpec((hr, d), const2)],
        out_shape=[jax.ShapeDtypeStruct((n, d), F32), jax.ShapeDtypeStruct((hr, d), F32)],
        scratch_shapes=[pltpu.VMEM((hr + rc, d), F32)],
        input_output_aliases=aliases,
        compiler_params=_params(),
        name="pool_mixer",
    )(*args)


def _gelu_tanh(x):
    return 0.5 * x * (1.0 + jnp.tanh(math.sqrt(2.0 / math.pi) * (x + 0.044715 * (x * x * x))))


def _ssm_kernel(x_ref, g_ref, h0r_ref, h0i_ref, ar_ref, ai_ref, wb_ref, wc_ref, dsk_ref, wglu_ref, alias_ref,
                o_ref, hr_out_ref, hi_out_ref, bu_ref, y_ref, hre_ref, him_ref, *, bt, tc):
    del alias_ref
    i = pl.program_id(0)
    d = x_ref.shape[1]
    n_slab = d // SSM_SLAB
    sw = hre_ref.shape[1] // n_slab

    @pl.when(i == 0)
    def _():
        hre_ref[...] = h0r_ref[...]
        him_ref[...] = h0i_ref[...]

    x = x_ref[...]
    u = _rms(x, g_ref[...])
    ub = u.astype(BF16)
    for s in range(n_slab):
        bu_ref[...] = jnp.dot(ub[:, s * SSM_SLAB:(s + 1) * SSM_SLAB], wb_ref[s], preferred_element_type=F32)
        ar = ar_ref[s]
        ai = ai_ref[s]
        for hb in range(bt // SUBLANES):
            b0 = hb * SUBLANES

            def step(t, carry, b0=b0, ar=ar, ai=ai):
                h_re, h_im = carry
                r0 = pl.multiple_of(t * bt + b0, SUBLANES)
                n_re = ar * h_re - ai * h_im + bu_ref[pl.ds(r0, SUBLANES), 0:sw]
                n_im = ar * h_im + ai * h_re + bu_ref[pl.ds(r0, SUBLANES), sw:2 * sw]
                bu_ref[pl.ds(r0, SUBLANES), 0:sw] = n_re
                bu_ref[pl.ds(r0, SUBLANES), sw:2 * sw] = n_im
                return n_re, n_im

            h_re, h_im = lax.fori_loop(
                0, tc, step,
                (hre_ref[b0:b0 + SUBLANES, s * sw:(s + 1) * sw], him_ref[b0:b0 + SUBLANES, s * sw:(s + 1) * sw]),
                unroll=4)
            hre_ref[b0:b0 + SUBLANES, s * sw:(s + 1) * sw] = h_re
            him_ref[b0:b0 + SUBLANES, s * sw:(s + 1) * sw] = h_im
        y_ref[:, s * SSM_SLAB:(s + 1) * SSM_SLAB] = jnp.dot(
            bu_ref[...].astype(BF16), wc_ref[s], preferred_element_type=F32)
    y = y_ref[...] + dsk_ref[...] * u
    gl = _gelu_tanh(y).astype(BF16)
    z = jnp.dot(gl, wglu_ref[...], preferred_element_type=F32)
    o_ref[...] = x + z[:, :d] * jax.nn.sigmoid(z[:, d:])
    hr_out_ref[...] = hre_ref[...]
    hi_out_ref[...] = him_ref[...]


def _ssm_call(x_all, prev_out, g, h0r, h0i, ar, ai, wb, wc, dsk, wglu, *, row0, n_rows, bt):
    n, d = x_all.shape
    rc = MIX_ROWS
    tc = rc // bt
    assert n_rows % rc == 0 and row0 % rc == 0 and bt % SUBLANES == 0
    blk0 = row0 // rc
    sc = h0r.shape[1]
    n_slab = d // SSM_SLAB
    kern = functools.partial(_ssm_kernel, bt=bt, tc=tc)
    const2 = lambda i: (0, 0)
    const3 = lambda i: (0, 0, 0)
    args = [x_all, g, h0r, h0i, ar, ai, wb, wc, dsk, wglu]
    in_specs = [
        pl.BlockSpec((rc, d), lambda i: (i + blk0, 0)),
        pl.BlockSpec((1, d), const2),
        pl.BlockSpec((bt, sc), const2),
        pl.BlockSpec((bt, sc), const2),
        pl.BlockSpec(ar.shape, const3),
        pl.BlockSpec(ai.shape, const3),
        pl.BlockSpec(wb.shape, const3, pipeline_mode=pl.Buffered(1)),
        pl.BlockSpec(wc.shape, const3, pipeline_mode=pl.Buffered(1)),
        pl.BlockSpec((1, d), const2),
        pl.BlockSpec(wglu.shape, const2, pipeline_mode=pl.Buffered(1)),
    ]
    aliases = {}
    if prev_out is None:
        prev_out = jnp.zeros((SUBLANES, LANES), F32)
    else:
        aliases = {10: 0}
    args.append(prev_out)
    in_specs.append(pl.BlockSpec(memory_space=pl.ANY))
    return pl.pallas_call(
        kern,
        grid=(n_rows // rc,),
        in_specs=in_specs,
        out_specs=[pl.BlockSpec((rc, d), lambda i: (i + blk0, 0)),
                   pl.BlockSpec((bt, sc), const2), pl.BlockSpec((bt, sc), const2)],
        out_shape=[jax.ShapeDtypeStruct((n, d), F32),
                   jax.ShapeDtypeStruct((bt, sc), F32), jax.ShapeDtypeStruct((bt, sc), F32)],
        scratch_shapes=[pltpu.VMEM((rc, 2 * sc // n_slab), F32), pltpu.VMEM((rc, d), F32),
                        pltpu.VMEM((bt, sc), F32), pltpu.VMEM((bt, sc), F32)],
        input_output_aliases=aliases,
        compiler_params=_params(),
        name="s5_mixer",
    )(*args)


def _ssm_weights(a_re, a_im, log_dt, b_re, b_im, c_re, c_im):
    g, p = a_re.shape
    h = b_re.shape[2]
    gs = SSM_SLAB // h
    n_slab = g // gs
    dt = jnp.exp(log_dt)[:, None]
    mag = jnp.exp(a_re * dt)
    abr = mag * jnp.cos(a_im * dt)
    abi = mag * jnp.sin(a_im * dt)
    den = a_re * a_re + a_im * a_im
    qr = ((abr - 1.0) * a_re + abi * a_im) / den
    qi = (abi * a_re - (abr - 1.0) * a_im) / den
    bbr = qr[..., None] * b_re - qi[..., None] * b_im
    bbi = qr[..., None] * b_im + qi[..., None] * b_re
    eye = jnp.eye(gs, dtype=F32)

    def in_blockdiag(m):
        m = m.reshape(n_slab, gs, p, h).transpose(0, 1, 3, 2)
        return jnp.einsum("ab,sahp->sahbp", eye, m).reshape(n_slab, gs * h, gs * p)

    def out_blockdiag(m):
        m = m.reshape(n_slab, gs, h, p).transpose(0, 1, 3, 2)
        return jnp.einsum("ab,saph->sapbh", eye, m).reshape(n_slab, gs * p, gs * h)

    wb = jnp.concatenate([in_blockdiag(bbr), in_blockdiag(bbi)], axis=-1).astype(BF16)
    wc = jnp.concatenate([out_blockdiag(c_re), out_blockdiag(-c_im)], axis=1).astype(BF16)
    ar = jnp.broadcast_to(abr.reshape(n_slab, 1, gs * p), (n_slab, SUBLANES, gs * p))
    ai = jnp.broadcast_to(abi.reshape(n_slab, 1, gs * p), (n_slab, SUBLANES, gs * p))
    return ar, ai, wb, wc


def _route_kernel(x_ref, g_ref, wr_ref, br_ref, gate_ref, lpos_ref, cnt_ref):
    tb = x_ref.shape[0]
    ne = wr_ref.shape[0]
    u = _rms(x_ref[...], g_ref[...])
    logits = lax.dot_general(wr_ref[...], u, (((1,), (1,)), ((), ())),
                             precision=lax.Precision.HIGHEST, preferred_element_type=F32) + br_ref[...]
    iota_e = lax.broadcasted_iota(jnp.int32, (ne, tb), 0)
    vals, ids = [], []
    l = logits
    for _ in range(TOP_K):
        m = jnp.max(l, axis=0, keepdims=True)
        idx = jnp.min(jnp.where(l == m, iota_e, ne), axis=0, keepdims=True)
        vals.append(m)
        ids.append(idx)
        l = jnp.where(iota_e == idx, -jnp.inf, l)
    ex = [jnp.exp(v - vals[0]) for v in vals]
    den = ex[0] + ex[1] + ex[2] + ex[3]
    gate_ref[0] = jnp.concatenate([e / den for e in ex], axis=0)

    member = jnp.zeros((ne, tb), F32)
    for idx in ids:
        member = jnp.where(iota_e == idx, 1.0, member)
    tri = (lax.broadcasted_iota(jnp.int32, (tb, tb), 0) <= lax.broadcasted_iota(jnp.int32, (tb, tb), 1))
    csum = jnp.dot(member.astype(BF16), tri.astype(BF16), preferred_element_type=F32)
    cnt = csum[:, tb - 1:tb]
    pcnt = jnp.floor((cnt + (SUBLANES - 1)) / SUBLANES) * SUBLANES
    ltri = (lax.broadcasted_iota(jnp.int32, (ne, ne), 1) < lax.broadcasted_iota(jnp.int32, (ne, ne), 0))
    loff = jnp.dot(ltri.astype(BF16), jnp.broadcast_to(pcnt, (ne, LANES)).astype(BF16),
                   preferred_element_type=F32)[:, 0:1]
    lpos = []
    for idx in ids:
        sel = iota_e == idx
        lpos.append(jnp.sum(jnp.where(sel, csum - 1.0 + loff, 0.0), axis=0, keepdims=True))
    lpos_ref[0] = jnp.concatenate(lpos, axis=0).astype(jnp.int32)
    cnt_ref[0] = jnp.broadcast_to(cnt, (ne, LANES)).astype(jnp.int32)


def _route_call(x, g, wr_t, br):
    n, d = x.shape
    tb = TOKEN_BLOCK
    nb = n // tb
    ne = wr_t.shape[0]
    const2 = lambda i: (0, 0)
    return pl.pallas_call(
        _route_kernel,
        grid=(nb,),
        in_specs=[pl.BlockSpec((tb, d), lambda i: (i, 0)), pl.BlockSpec((1, d), const2),
                  pl.BlockSpec((ne, d), const2), pl.BlockSpec((ne, 1), const2)],
        out_specs=[pl.BlockSpec((1, TOP_K, tb), lambda i: (i, 0, 0)),
                   pl.BlockSpec((1, TOP_K, tb), lambda i: (i, 0, 0)),
                   pl.BlockSpec((1, ne, LANES), lambda i: (i, 0, 0))],
        out_shape=[jax.ShapeDtypeStruct((nb, TOP_K, tb), F32),
                   jax.ShapeDtypeStruct((nb, TOP_K, tb), jnp.int32),
                   jax.ShapeDtypeStruct((nb, ne, LANES), jnp.int32)],
        compiler_params=_params(),
        name="moe_route",
    )(x, g, wr_t, br)


def _local_rows(ne):
    worst = TOKEN_BLOCK * TOP_K + ne * (SUBLANES - 1)
    return -(-worst // LANES) * LANES


def _run_copies(n8, src_ref, src0, dst_ref, dst0, sem, *, wait):
    top = (TOKEN_BLOCK // SUBLANES).bit_length() - 1
    for bit in range(top, -1, -1):
        size = SUBLANES << bit
        off = (n8 >> (bit + 1)) << (bit + 1 + 3)

        @pl.when(((n8 >> bit) & 1) == 1)
        def _(size=size, off=off):
            cp = pltpu.make_async_copy(
                src_ref.at[pl.ds(pl.multiple_of(src0 + off, SUBLANES), size)],
                dst_ref.at[pl.ds(pl.multiple_of(dst0 + off, SUBLANES), size)], sem)
            if wait:
                cp.wait()
            else:
                cp.start()


def _sort_kernel(pc8_ref, loff_ref, gst_ref, x_ref, g_ref, lpos_ref, xs_ref, buf_ref, sem, *, ne):
    b = pl.program_id(0)
    nb = pl.num_programs(0)
    tb = x_ref.shape[0]
    lr = buf_ref.shape[0]
    ub = _rms(x_ref[...], g_ref[...]).astype(BF16)
    lp = lpos_ref[0]
    iota_r = lax.broadcasted_iota(jnp.int32, (lr, tb), 0)
    perm = jnp.zeros((lr, tb), F32)
    for k in range(TOP_K):
        perm = jnp.where(iota_r == lp[k:k + 1, :], 1.0, perm)
    xs_local = jnp.dot(perm.astype(BF16), ub, preferred_element_type=F32)

    def copies(blk, wait):
        def body(e, c):
            j = blk * ne + e
            _run_copies(pc8_ref[j], buf_ref, loff_ref[j], xs_ref, gst_ref[j], sem, wait=wait)
            return c
        lax.fori_loop(0, ne, body, 0)

    @pl.when(b > 0)
    def _():
        copies(b - 1, True)

    buf_ref[...] = xs_local
    copies(b, False)

    @pl.when(b == nb - 1)
    def _():
        copies(b, True)


def _sort_call(x, g, lpos, pc8, loff, gst, *, n_rows_out, ne):
    n, d = x.shape
    tb = TOKEN_BLOCK
    nb = n // tb
    lr = _local_rows(ne)
    gs = pltpu.PrefetchScalarGridSpec(
        num_scalar_prefetch=3,
        grid=(nb,),
        in_specs=[pl.BlockSpec((tb, d), lambda i, *_: (i, 0)), pl.BlockSpec((1, d), lambda i, *_: (0, 0)),
                  pl.BlockSpec((1, TOP_K, tb), lambda i, *_: (i, 0, 0))],
        out_specs=pl.BlockSpec(memory_space=pl.ANY),
        scratch_shapes=[pltpu.VMEM((lr, d), F32), pltpu.SemaphoreType.DMA(())],
    )
    return pl.pallas_call(
        functools.partial(_sort_kernel, ne=ne),
        grid_spec=gs,
        out_shape=jax.ShapeDtypeStruct((n_rows_out, d), F32),
        compiler_params=_params(has_side_effects=True),
        name="moe_sort",
    )(pc8, loff, gst, x, g, lpos)


def _ffn_kernel(te_ref, tr_ref, nt_ref, xs_ref, wup_ref, bup_ref, wdn_ref, bdn_ref, ys_ref, wup_bf, wdn_bf):
    i = pl.program_id(0)
    f = wdn_ref.shape[0]
    live = i < nt_ref[0]
    prev = te_ref[jnp.maximum(i - 1, 0)]
    fresh = jnp.logical_or(i == 0, prev != te_ref[i])

    @pl.when(jnp.logical_and(live, fresh))
    def _():
        wup_bf[...] = wup_ref[...].astype(BF16)
        wdn_bf[...] = wdn_ref[...].astype(BF16)

    @pl.when(live)
    def _():
        x = xs_ref[...].astype(BF16)
        gu = jnp.dot(x, wup_bf[...], preferred_element_type=F32) + bup_ref[...]
        glu = jnp.minimum(gu[:, :f], SWIGLU_LIMIT)
        lin = jnp.clip(gu[:, f:], -SWIGLU_LIMIT, SWIGLU_LIMIT)
        act = glu * jax.nn.sigmoid(SWIGLU_ALPHA * glu) * (lin + 1.0)
        ys_ref[...] = jnp.dot(act.astype(BF16), wdn_bf[...], preferred_element_type=F32) + bdn_ref[...]


def _ffn_call(xs, w_up, b_up, w_dn, b_dn, tile_expert, tile_row, n_tiles, *, layer):
    r, d = xs.shape
    tm = FFN_TILE
    _, ne, _, f2 = w_up.shape
    f = w_dn.shape[2]
    gs = pltpu.PrefetchScalarGridSpec(
        num_scalar_prefetch=3,
        grid=(r // tm,),
        in_specs=[
            pl.BlockSpec((tm, d), lambda i, te, tr, nt: (tr[i], 0)),
            pl.BlockSpec((None, None, d, f2), lambda i, te, tr, nt: (layer, te[i], 0, 0)),
            pl.BlockSpec((None, None, 1, f2), lambda i, te, tr, nt: (layer, te[i], 0, 0)),
            pl.BlockSpec((None, None, f, d), lambda i, te, tr, nt: (layer, te[i], 0, 0)),
            pl.BlockSpec((None, None, 1, d), lambda i, te, tr, nt: (layer, te[i], 0, 0)),
        ],
        out_specs=pl.BlockSpec((tm, d), lambda i, te, tr, nt: (tr[i], 0)),
        scratch_shapes=[pltpu.VMEM((d, f2), BF16), pltpu.VMEM((f, d), BF16)],
    )
    return pl.pallas_call(
        _ffn_kernel,
        grid_spec=gs,
        out_shape=jax.ShapeDtypeStruct((r, d), F32),
        compiler_params=_params(),
        name="moe_ffn",
    )(tile_expert, tile_row, n_tiles, xs, w_up, b_up[:, :, None, :], w_dn, b_dn[:, :, None, :])


def _combine_kernel(pc8_ref, loff_ref, gst_ref, used_ref, x_ref, lpos_ref, gate_ref, gfin_ref, ys_ref, o_ref,
                    buf_ref, sem, *, ne, final_norm):
    b = pl.program_id(0)
    nb = pl.num_programs(0)
    tb = x_ref.shape[0]
    lr = buf_ref.shape[1]

    def copies(blk, slot, wait):
        def body(e, c):
            j = blk * ne + e
            _run_copies(pc8_ref[j], ys_ref, gst_ref[j], buf_ref.at[slot], loff_ref[j], sem.at[slot], wait=wait)
            return c
        lax.fori_loop(0, ne, body, 0)

    slot = b % 2

    @pl.when(b == 0)
    def _():
        buf_ref[...] = jnp.zeros(buf_ref.shape, F32)
        copies(b, slot, False)

    @pl.when(b + 1 < nb)
    def _():
        copies(b + 1, 1 - slot, False)

    lp = lpos_ref[0]
    gt = gate_ref[0]
    iota_r = lax.broadcasted_iota(jnp.int32, (lr, tb), 0)
    wperm = jnp.zeros((lr, tb), F32)
    for k in range(TOP_K):
        wperm = jnp.where(iota_r == lp[k:k + 1, :], gt[k:k + 1, :], wperm)

    copies(b, slot, True)
    row = lax.broadcasted_iota(jnp.int32, (lr, 1), 0)
    ys_local = jnp.where(row < used_ref[b], buf_ref[slot], 0.0).astype(BF16)
    out = lax.dot_general(wperm.astype(BF16), ys_local, (((0,), (0,)), ((), ())), preferred_element_type=F32)
    y = x_ref[...] + out
    if final_norm:
        y = _rms(y, gfin_ref[...])
    o_ref[...] = y


def _combine_call(x, lpos, gate, gfin, ys, pc8, loff, gst, used, *, ne, final_norm):
    n, d = x.shape
    tb = TOKEN_BLOCK
    nb = n // tb
    lr = _local_rows(ne)
    gs = pltpu.PrefetchScalarGridSpec(
        num_scalar_prefetch=4,
        grid=(nb,),
        in_specs=[pl.BlockSpec((tb, d), lambda i, *_: (i, 0)),
                  pl.BlockSpec((1, TOP_K, tb), lambda i, *_: (i, 0, 0)),
                  pl.BlockSpec((1, TOP_K, tb), lambda i, *_: (i, 0, 0)),
                  pl.BlockSpec((1, d), lambda i, *_: (0, 0)),
                  pl.BlockSpec(memory_space=pl.ANY)],
        out_specs=pl.BlockSpec((tb, d), lambda i, *_: (i, 0)),
        scratch_shapes=[pltpu.VMEM((2, lr, d), F32), pltpu.SemaphoreType.DMA((2,))],
    )
    return pl.pallas_call(
        functools.partial(_combine_kernel, ne=ne, final_norm=final_norm),
        grid_spec=gs,
        out_shape=jax.ShapeDtypeStruct((n, d), F32),
        compiler_params=_params(),
        name="moe_combine",
    )(pc8, loff, gst, used, x, lpos, gate, gfin, ys)


def _moe_layer(x, g_ffn, w_r, b_r, w_up, b_up, w_dn, b_dn, g_final, *, layer, final_norm):
    n, d = x.shape
    ne = w_r.shape[1]
    tb, tm = TOKEN_BLOCK, FFN_TILE
    nb = n // tb
    gate, lpos, cnt = _route_call(x, g_ffn, w_r.T, b_r[:, None])
    cnt = cnt[:, :, 0]
    pc = (cnt + (SUBLANES - 1)) // SUBLANES * SUBLANES
    loff = jnp.cumsum(pc, axis=1) - pc
    rows_e = jnp.sum(pc, axis=0)
    tiles_e = (rows_e + tm - 1) // tm
    tile_end = jnp.cumsum(tiles_e)
    gst = ((tile_end - tiles_e) * tm)[None, :] + jnp.cumsum(pc, axis=0) - pc
    n_tiles = tile_end[-1]
    max_rows = n * TOP_K + nb * ne * (SUBLANES - 1)
    max_tiles = (max_rows + ne * (tm - SUBLANES)) // tm
    t = jnp.arange(max_tiles, dtype=jnp.int32)
    t_live = jnp.minimum(t, n_tiles - 1)
    tile_expert = jnp.sum(t_live[:, None] >= tile_end[None, :], axis=1).astype(jnp.int32)
    i32 = lambda a: a.astype(jnp.int32).reshape(-1)
    pc8, loff, gst = i32(pc // SUBLANES), i32(loff), i32(gst)
    used = i32(jnp.sum(pc, axis=1))
    xs = _sort_call(x, g_ffn, lpos, pc8, loff, gst, n_rows_out=max_tiles * tm, ne=ne)
    ys = _ffn_call(xs, w_up, b_up, w_dn, b_dn, tile_expert, i32(t_live), i32(n_tiles), layer=layer)
    return _combine_call(x, lpos, gate, g_final, ys, pc8, loff, gst, used, ne=ne, final_norm=final_norm)


def _time_major(a):
    b, l, d = a.shape
    return a.transpose(1, 0, 2).reshape(l * b, d)


def _batch_major(a, b):
    n, d = a.shape
    return a.reshape(n // b, b, d).transpose(1, 0, 2)


def kernel(x_prompt, x_sample, cache_pool, state_ssm_re, state_ssm_im, norm_mix_g, norm_ffn_g, norm_final_g,
           pool_w, pool_scale, ssm_a_re, ssm_a_im, ssm_log_dt, ssm_b_re, ssm_b_im, ssm_c_re, ssm_c_im,
           ssm_d, ssm_glu_w, ssm_glu_gate, router_w, router_b, moe_w_up, moe_b_up, moe_w_down, moe_b_down):
    bp, lp, d = x_prompt.shape
    bs, ls, _ = x_sample.shape
    depth = norm_mix_g.shape[0]
    assert depth == 2 and cache_pool.shape[0] == 1 and state_ssm_re.shape[0] == 1
    n_p, n_s = bp * lp, bs * ls
    row = lambda v: v.reshape(1, -1)

    x = jnp.concatenate([_time_major(x_prompt), _time_major(x_sample)], axis=0)

    pw = pool_w[0].astype(BF16)
    hist_p = jnp.zeros((POOL_HIST * bp, d), F32)
    hist_s = _time_major(cache_pool[0])
    x1, pool_p = _pool_call(x, None, hist_p, row(norm_mix_g[0]), pw, row(pool_scale[0]),
                            row0=0, n_rows=n_p, bt=bp, start_pos=0)
    x1, pool_s = _pool_call(x, x1, hist_s, row(norm_mix_g[0]), pw, row(pool_scale[0]),
                            row0=n_p, n_rows=n_s, bt=bs, start_pos=PAST_LEN)
    x2 = _moe_layer(x1, row(norm_ffn_g[0]), router_w[0], router_b[0], moe_w_up, moe_b_up, moe_w_down, moe_b_down,
                    row(norm_final_g), layer=0, final_norm=False)

    ar, ai, wb, wc = _ssm_weights(ssm_a_re[0], ssm_a_im[0], ssm_log_dt[0], ssm_b_re[0], ssm_b_im[0],
                                  ssm_c_re[0], ssm_c_im[0])
    wglu = jnp.concatenate([ssm_glu_w[0], ssm_glu_gate[0]], axis=1).astype(BF16)
    g_groups, p_state = ssm_a_re.shape[1:]
    sc = g_groups * p_state
    zero_state = jnp.zeros((bp, sc), F32)
    x3, re_p, im_p = _ssm_call(x2, None, row(norm_mix_g[1]), zero_state, zero_state, ar, ai, wb, wc,
                               row(ssm_d[0]), wglu, row0=0, n_rows=n_p, bt=bp)
    x3, re_s, im_s = _ssm_call(x2, x3, row(norm_mix_g[1]), state_ssm_re[0].reshape(bs, sc),
                               state_ssm_im[0].reshape(bs, sc), ar, ai, wb, wc,
                               row(ssm_d[0]), wglu, row0=n_p, n_rows=n_s, bt=bs)
    y = _moe_layer(x3, row(norm_ffn_g[1]), router_w[1], router_b[1], moe_w_up, moe_b_up, moe_w_down, moe_b_down,
                   row(norm_final_g), layer=1, final_norm=True)

    st = lambda a, b: a.reshape(1, b, g_groups, p_state)
    return (_batch_major(y[:n_p], bp), _batch_major(y[n_p:], bs),
            _batch_major(pool_p, bp)[None], _batch_major(pool_s, bs)[None],
            st(re_p, bp), st(im_p, bp), st(re_s, bs), st(im_s, bs))
```

```python
import functools
import math

import jax
import jax.numpy as jnp
from jax import lax
from jax.experimental import pallas as pl
from jax.experimental.pallas import tpu as pltpu

F32 = jnp.float32
BF16 = jnp.bfloat16

POOL_WINDOWS = (2, 4, 8, 16)
POOL_HIST = max(POOL_WINDOWS) - 1
SSM_GROUP_CH = 16
SSM_STATE = 64
TOP_K = 4
SWIGLU_LIMIT = 7.0
SWIGLU_ALPHA = 1.702
RMS_EPS = 1e-5
PAST_LEN = 1024

SUBLANES = 8
LANES = 128
MXU_DIM = 256
VMEM_LIMIT = 56 * 1024 * 1024

MIX_ROWS = 512
TOKEN_BLOCK = 256
FFN_TILE = 256
SSM_SLAB = MXU_DIM


def _rms(x, g):
    return x * lax.rsqrt(jnp.mean(x * x, axis=-1, keepdims=True) + RMS_EPS) * g


def _params(**kw):
    return pltpu.CompilerParams(dimension_semantics=("arbitrary",), vmem_limit_bytes=VMEM_LIMIT, **kw)


def _pool_kernel(x_ref, hist_ref, g_ref, w_ref, scale_ref, alias_ref, o_ref, hist_out_ref, ext_ref,
                 *, bt, tc, start_pos):
    del alias_ref
    i = pl.program_id(0)
    rc = tc * bt
    hr = POOL_HIST * bt
    gw = x_ref.shape[1] // len(POOL_WINDOWS)

    @pl.when(i == 0)
    def _():
        ext_ref[0:hr, :] = hist_ref[...]

    x = x_ref[...]
    u = _rms(x, g_ref[...])
    ext_ref[hr:hr + rc, :] = u
    row = lax.broadcasted_iota(jnp.int32, (rc, 1), 0)
    pos = start_pos + i * tc + row // bt
    outs = []
    for gi, w in enumerate(POOL_WINDOWS):
        c0 = gi * gw
        s = ext_ref[hr:hr + rc, c0:c0 + gw]
        for j in range(1, w):
            s = s + ext_ref[hr - j * bt:hr - j * bt + rc, c0:c0 + gw]
        cnt = jnp.minimum(w, pos + 1).astype(F32)
        mixed = s / cnt - u[:, c0:c0 + gw]
        outs.append(jnp.dot(mixed.astype(BF16), w_ref[gi], preferred_element_type=F32))
    y = jnp.concatenate(outs, axis=-1) * scale_ref[...]
    o_ref[...] = x + y
    ext_ref[0:hr, :] = ext_ref[rc:rc + hr, :]
    hist_out_ref[...] = ext_ref[0:hr, :]


def _pool_call(x_all, prev_out, hist, g, w_bf, scale, *, row0, n_rows, bt, start_pos):
    n, d = x_all.shape
    rc = MIX_ROWS
    tc = rc // bt
    hr = POOL_HIST * bt
    assert n_rows % rc == 0 and row0 % rc == 0 and rc > hr
    blk0 = row0 // rc
    kern = functools.partial(_pool_kernel, bt=bt, tc=tc, start_pos=start_pos)
    const2 = lambda i: (0, 0)
    args = [x_all, hist, g, w_bf, scale]
    in_specs = [
        pl.BlockSpec((rc, d), lambda i: (i + blk0, 0)),
        pl.BlockSpec((hr, d), const2),
        pl.BlockSpec((1, d), const2),
        pl.BlockSpec(w_bf.shape, lambda i: (0, 0, 0)),
        pl.BlockSpec((1, d), const2),
    ]
    aliases = {}
    if prev_out is None:
        prev_out = jnp.zeros((SUBLANES, LANES), F32)
    else:
        aliases = {5: 0}
    args.append(prev_out)
    in_specs.append(pl.BlockSpec(memory_space=pl.ANY))
    return pl.pallas_call(
        kern,
        grid=(n_rows // rc,),
        in_specs=in_specs,
        out_specs=[pl.BlockSpec((rc, d), lambda i: (i + blk0, 0)), pl.BlockSpec((hr, d), const2)],
        out_shape=[jax.ShapeDtypeStruct((n, d), F32), jax.ShapeDtypeStruct((hr, d), F32)],
        scratch_shapes=[pltpu.VMEM((hr + rc, d), F32)],
        input_output_aliases=aliases,
        compiler_params=_params(),
        name="pool_mixer",
    )(*args)


def _gelu_tanh(x):
    return 0.5 * x * (1.0 + jnp.tanh(math.sqrt(2.0 / math.pi) * (x + 0.044715 * (x * x * x))))


def _ssm_kernel(x_ref, g_ref, h0r_ref, h0i_ref, ar_ref, ai_ref, wb_ref, wc_ref, dsk_ref, wglu_ref, alias_ref,
                o_ref, hr_out_ref, hi_out_ref, bu_ref, y_ref, hre_ref, him_ref, *, bt, tc):
    del alias_ref
    i = pl.program_id(0)
    d = x_ref.shape[1]
    n_slab = d // SSM_SLAB
    sw = hre_ref.shape[1] // n_slab

    @pl.when(i == 0)
    def _():
        hre_ref[...] = h0r_ref[...]
        him_ref[...] = h0i_ref[...]

    x = x_ref[...]
    u = _rms(x, g_ref[...])
    ub = u.astype(BF16)
    for s in range(n_slab):
        bu_ref[...] = jnp.dot(ub[:, s * SSM_SLAB:(s + 1) * SSM_SLAB], wb_ref[s], preferred_element_type=F32)
        ar = ar_ref[s]
        ai = ai_ref[s]
        for hb in range(bt // SUBLANES):
            b0 = hb * SUBLANES

            def step(t, carry, b0=b0, ar=ar, ai=ai):
                h_re, h_im = carry
                r0 = pl.multiple_of(t * bt + b0, SUBLANES)
                n_re = ar * h_re - ai * h_im + bu_ref[pl.ds(r0, SUBLANES), 0:sw]
                n_im = ar * h_im + ai * h_re + bu_ref[pl.ds(r0, SUBLANES), sw:2 * sw]
                bu_ref[pl.ds(r0, SUBLANES), 0:sw] = n_re
                bu_ref[pl.ds(r0, SUBLANES), sw:2 * sw] = n_im
                return n_re, n_im

            h_re, h_im = lax.fori_loop(
                0, tc, step,
                (hre_ref[b0:b0 + SUBLANES, s * sw:(s + 1) * sw], him_ref[b0:b0 + SUBLANES, s * sw:(s + 1) * sw]),
                unroll=4)
            hre_ref[b0:b0 + SUBLANES, s * sw:(s + 1) * sw] = h_re
            him_ref[b0:b0 + SUBLANES, s * sw:(s + 1) * sw] = h_im
        y_ref[:, s * SSM_SLAB:(s + 1) * SSM_SLAB] = jnp.dot(
            bu_ref[...].astype(BF16), wc_ref[s], preferred_element_type=F32)
    y = y_ref[...] + dsk_ref[...] * u
    gl = _gelu_tanh(y).astype(BF16)
    z = jnp.dot(gl, wglu_ref[...], preferred_element_type=F32)
    o_ref[...] = x + z[:, :d] * jax.nn.sigmoid(z[:, d:])
    hr_out_ref[...] = hre_ref[...]
    hi_out_ref[...] = him_ref[...]


def _ssm_call(x_all, prev_out, g, h0r, h0i, ar, ai, wb, wc, dsk, wglu, *, row0, n_rows, bt):
    n, d = x_all.shape
    rc = MIX_ROWS
    tc = rc // bt
    assert n_rows % rc == 0 and row0 % rc == 0 and bt % SUBLANES == 0
    blk0 = row0 // rc
    sc = h0r.shape[1]
    n_slab = d // SSM_SLAB
    kern = functools.partial(_ssm_kernel, bt=bt, tc=tc)
    const2 = lambda i: (0, 0)
    const3 = lambda i: (0, 0, 0)
    args = [x_all, g, h0r, h0i, ar, ai, wb, wc, dsk, wglu]
    in_specs = [
        pl.BlockSpec((rc, d), lambda i: (i + blk0, 0)),
        pl.BlockSpec((1, d), const2),
        pl.BlockSpec((bt, sc), const2),
        pl.BlockSpec((bt, sc), const2),
        pl.BlockSpec(ar.shape, const3),
        pl.BlockSpec(ai.shape, const3),
        pl.BlockSpec(wb.shape, const3, pipeline_mode=pl.Buffered(1)),
        pl.BlockSpec(wc.shape, const3, pipeline_mode=pl.Buffered(1)),
        pl.BlockSpec((1, d), const2),
        pl.BlockSpec(wglu.shape, const2, pipeline_mode=pl.Buffered(1)),
    ]
    aliases = {}
    if prev_out is None:
        prev_out = jnp.zeros((SUBLANES, LANES), F32)
    else:
        aliases = {10: 0}
    args.append(prev_out)
    in_specs.append(pl.BlockSpec(memory_space=pl.ANY))
    return pl.pallas_call(
        kern,
        grid=(n_rows // rc,),
        in_specs=in_specs,
        out_specs=[pl.BlockSpec((rc, d), lambda i: (i + blk0, 0)),
                   pl.BlockSpec((bt, sc), const2), pl.BlockSpec((bt, sc), const2)],
        out_shape=[jax.ShapeDtypeStruct((n, d), F32),
                   jax.ShapeDtypeStruct((bt, sc), F32), jax.ShapeDtypeStruct((bt, sc), F32)],
        scratch_shapes=[pltpu.VMEM((rc, 2 * sc // n_slab), F32), pltpu.VMEM((rc, d), F32),
                        pltpu.VMEM((bt, sc), F32), pltpu.VMEM((bt, sc), F32)],
        input_output_aliases=aliases,
        compiler_params=_params(),
        name="s5_mixer",
    )(*args)


def _ssm_weights(a_re, a_im, log_dt, b_re, b_im, c_re, c_im):
    g, p = a_re.shape
    h = b_re.shape[2]
    gs = SSM_SLAB // h
    n_slab = g // gs
    dt = jnp.exp(log_dt)[:, None]
    mag = jnp.exp(a_re * dt)
    abr = mag * jnp.cos(a_im * dt)
    abi = mag * jnp.sin(a_im * dt)
    den = a_re * a_re + a_im * a_im
    qr = ((abr - 1.0) * a_re + abi * a_im) / den
    qi = (abi * a_re - (abr - 1.0) * a_im) / den
    bbr = qr[..., None] * b_re - qi[..., None] * b_im
    bbi = qr[..., None] * b_im + qi[..., None] * b_re
    eye = jnp.eye(gs, dtype=F32)

    def in_blockdiag(m):
        m = m.reshape(n_slab, gs, p, h).transpose(0, 1, 3, 2)
        return jnp.einsum("ab,sahp->sahbp", eye, m).reshape(n_slab, gs * h, gs * p)

    def out_blockdiag(m):
        m = m.reshape(n_slab, gs, h, p).transpose(0, 1, 3, 2)
        return jnp.einsum("ab,saph->sapbh", eye, m).reshape(n_slab, gs * p, gs * h)

    wb = jnp.concatenate([in_blockdiag(bbr), in_blockdiag(bbi)], axis=-1).astype(BF16)
    wc = jnp.concatenate([out_blockdiag(c_re), out_blockdiag(-c_im)], axis=1).astype(BF16)
    ar = jnp.broadcast_to(abr.reshape(n_slab, 1, gs * p), (n_slab, SUBLANES, gs * p))
    ai = jnp.broadcast_to(abi.reshape(n_slab, 1, gs * p), (n_slab, SUBLANES, gs * p))
    return ar, ai, wb, wc


def _route_kernel(x_ref, g_ref, wr_ref, br_ref, gate_ref, lpos_ref, cnt_ref):
    tb = x_ref.shape[0]
    ne = wr_ref.shape[0]
    u = _rms(x_ref[...], g_ref[...])
    logits = lax.dot_general(wr_ref[...], u, (((1,), (1,)), ((), ())),
                             precision=lax.Precision.HIGHEST, preferred_element_type=F32) + br_ref[...]
    iota_e = lax.broadcasted_iota(jnp.int32, (ne, tb), 0)
    vals, ids = [], []
    l = logits
    for _ in range(TOP_K):
        m = jnp.max(l, axis=0, keepdims=True)
        idx = jnp.min(jnp.where(l == m, iota_e, ne), axis=0, keepdims=True)
        vals.append(m)
        ids.append(idx)
        l = jnp.where(iota_e == idx, -jnp.inf, l)
    ex = [jnp.exp(v - vals[0]) for v in vals]
    den = ex[0] + ex[1] + ex[2] + ex[3]
    gate_ref[0] = jnp.concatenate([e / den for e in ex], axis=0)

    member = jnp.zeros((ne, tb), F32)
    for idx in ids:
        member = jnp.where(iota_e == idx, 1.0, member)
    tri = (lax.broadcasted_iota(jnp.int32, (tb, tb), 0) <= lax.broadcasted_iota(jnp.int32, (tb, tb), 1))
    csum = jnp.dot(member.astype(BF16), tri.astype(BF16), preferred_element_type=F32)
    cnt = csum[:, tb - 1:tb]
    pcnt = jnp.floor((cnt + (SUBLANES - 1)) / SUBLANES) * SUBLANES
    ltri = (lax.broadcasted_iota(jnp.int32, (ne, ne), 1) < lax.broadcasted_iota(jnp.int32, (ne, ne), 0))
    loff = jnp.dot(ltri.astype(BF16), jnp.broadcast_to(pcnt, (ne, LANES)).astype(BF16),
                   preferred_element_type=F32)[:, 0:1]
    lpos = []
    for idx in ids:
        sel = iota_e == idx
        lpos.append(jnp.sum(jnp.where(sel, csum - 1.0 + loff, 0.0), axis=0, keepdims=True))
    lpos_ref[0] = jnp.concatenate(lpos, axis=0).astype(jnp.int32)
    cnt_ref[0] = jnp.broadcast_to(cnt, (ne, LANES)).astype(jnp.int32)


def _route_call(x, g, wr_t, br):
    n, d = x.shape
    tb = TOKEN_BLOCK
    nb = n // tb
    ne = wr_t.shape[0]
    const2 = lambda i: (0, 0)
    return pl.pallas_call(
        _route_kernel,
        grid=(nb,),
        in_specs=[pl.BlockSpec((tb, d), lambda i: (i, 0)), pl.BlockSpec((1, d), const2),
                  pl.BlockSpec((ne, d), const2), pl.BlockSpec((ne, 1), const2)],
        out_specs=[pl.BlockSpec((1, TOP_K, tb), lambda i: (i, 0, 0)),
                   pl.BlockSpec((1, TOP_K, tb), lambda i: (i, 0, 0)),
                   pl.BlockSpec((1, ne, LANES), lambda i: (i, 0, 0))],
        out_shape=[jax.ShapeDtypeStruct((nb, TOP_K, tb), F32),
                   jax.ShapeDtypeStruct((nb, TOP_K, tb), jnp.int32),
                   jax.ShapeDtypeStruct((nb, ne, LANES), jnp.int32)],
        compiler_params=_params(),
        name="moe_route",
    )(x, g, wr_t, br)


def _local_rows(ne):
    worst = TOKEN_BLOCK * TOP_K + ne * (SUBLANES - 1)
    return -(-worst // LANES) * LANES


def _run_copies(n8, src_ref, src0, dst_ref, dst0, sem):
    top = (TOKEN_BLOCK // SUBLANES).bit_length() - 1
    for bit in range(top, -1, -1):
        size = SUBLANES << bit
        off = (n8 >> (bit + 1)) << (bit + 1 + 3)

        @pl.when(((n8 >> bit) & 1) == 1)
        def _(size=size, off=off):
            pltpu.make_async_copy(
                src_ref.at[pl.ds(pl.multiple_of(src0 + off, SUBLANES), size)],
                dst_ref.at[pl.ds(pl.multiple_of(dst0 + off, SUBLANES), size)], sem).start()


def _wait_rows(n_rows, src_ref, dst_ref, sem):
    n8 = n_rows // SUBLANES
    top = (min(src_ref.shape[0], dst_ref.shape[0]) // SUBLANES).bit_length() - 1
    for bit in range(top, -1, -1):
        size = SUBLANES << bit

        @pl.when(((n8 >> bit) & 1) == 1)
        def _(size=size):
            pltpu.make_async_copy(src_ref.at[pl.ds(0, size)], dst_ref.at[pl.ds(0, size)], sem).wait()


def _sort_kernel(pc8_ref, loff_ref, gst_ref, used_ref, x_ref, g_ref, lpos_ref, xs_ref, buf_ref, sem, *, ne):
    b = pl.program_id(0)
    nb = pl.num_programs(0)
    tb = x_ref.shape[0]
    lr = buf_ref.shape[1]
    slot = b % 2
    ub = _rms(x_ref[...], g_ref[...]).astype(BF16)
    lp = lpos_ref[0]
    iota_r = lax.broadcasted_iota(jnp.int32, (lr, tb), 0)
    perm = jnp.zeros((lr, tb), F32)
    for k in range(TOP_K):
        perm = jnp.where(iota_r == lp[k:k + 1, :], 1.0, perm)

    def drain(blk):
        _wait_rows(used_ref[blk], buf_ref.at[blk % 2], xs_ref, sem.at[blk % 2])

    @pl.when(b >= 2)
    def _():
        drain(b - 2)

    buf_ref[slot] = jnp.dot(perm.astype(BF16), ub, preferred_element_type=F32)

    def start_run(e, c):
        j = b * ne + e
        _run_copies(pc8_ref[j], buf_ref.at[slot], loff_ref[j], xs_ref, gst_ref[j], sem.at[slot])
        return c

    lax.fori_loop(0, ne, start_run, 0)

    @pl.when(b == nb - 1)
    def _():
        @pl.when(b >= 1)
        def _():
            drain(b - 1)

        drain(b)


def _sort_call(x, g, lpos, pc8, loff, gst, used, *, n_rows_out, ne):
    n, d = x.shape
    tb = TOKEN_BLOCK
    nb = n // tb
    lr = _local_rows(ne)
    gs = pltpu.PrefetchScalarGridSpec(
        num_scalar_prefetch=4,
        grid=(nb,),
        in_specs=[pl.BlockSpec((tb, d), lambda i, *_: (i, 0)), pl.BlockSpec((1, d), lambda i, *_: (0, 0)),
                  pl.BlockSpec((1, TOP_K, tb), lambda i, *_: (i, 0, 0))],
        out_specs=pl.BlockSpec(memory_space=pl.ANY),
        scratch_shapes=[pltpu.VMEM((2, lr, d), F32), pltpu.SemaphoreType.DMA((2,))],
    )
    return pl.pallas_call(
        functools.partial(_sort_kernel, ne=ne),
        grid_spec=gs,
        out_shape=jax.ShapeDtypeStruct((n_rows_out, d), F32),
        compiler_params=_params(has_side_effects=True),
        name="moe_sort",
    )(pc8, loff, gst, used, x, g, lpos)


def _ffn_kernel(t0_ref, nt_ref, xs_ref, wup_ref, bup_ref, wdn_ref, bdn_ref, ys_ref,
                wup_bf, wdn_bf, xbuf, ybuf, sem_in, sem_out):
    e = pl.program_id(0)
    ne = pl.num_programs(0)
    tm = xbuf.shape[1]
    f = wdn_ref.shape[0]
    t0 = t0_ref[e]
    nt = nt_ref[e]
    total = t0_ref[ne - 1] + nt_ref[ne - 1]

    def fetch(g):
        return pltpu.make_async_copy(xs_ref.at[pl.ds(pl.multiple_of(g * tm, tm), tm)], xbuf.at[g % 2],
                                     sem_in.at[g % 2])

    def writeback(g):
        return pltpu.make_async_copy(ybuf.at[g % 2], ys_ref.at[pl.ds(pl.multiple_of(g * tm, tm), tm)],
                                     sem_out.at[g % 2])

    @pl.when(e == 0)
    def _():
        fetch(0).start()

    @pl.when(nt > 0)
    def _():
        wup_bf[...] = wup_ref[...].astype(BF16)
        wdn_bf[...] = wdn_ref[...].astype(BF16)

    def tile(j, c):
        g = t0 + j
        fetch(g).wait()

        @pl.when(g + 1 < total)
        def _():
            fetch(g + 1).start()

        @pl.when(g >= 2)
        def _():
            writeback(g - 2).wait()

        x = xbuf[g % 2].astype(BF16)
        gu = jnp.dot(x, wup_bf[...], preferred_element_type=F32) + bup_ref[...]
        glu = jnp.minimum(gu[:, :f], SWIGLU_LIMIT)
        lin = jnp.clip(gu[:, f:], -SWIGLU_LIMIT, SWIGLU_LIMIT)
        act = glu * jax.nn.sigmoid(SWIGLU_ALPHA * glu) * (lin + 1.0)
        ybuf[g % 2] = jnp.dot(act.astype(BF16), wdn_bf[...], preferred_element_type=F32) + bdn_ref[...]
        writeback(g).start()
        return c

    lax.fori_loop(0, nt, tile, 0)

    @pl.when(e == ne - 1)
    def _():
        @pl.when(total >= 2)
        def _():
            writeback(total - 2).wait()

        writeback(total - 1).wait()


def _ffn_call(xs, w_up, b_up, w_dn, b_dn, tile_start, tiles_e, *, layer):
    r, d = xs.shape
    tm = FFN_TILE
    _, ne, _, f2 = w_up.shape
    f = w_dn.shape[2]
    by_expert = lambda e, t0, nt: (layer, e, 0, 0)
    gs = pltpu.PrefetchScalarGridSpec(
        num_scalar_prefetch=2,
        grid=(ne,),
        in_specs=[
            pl.BlockSpec(memory_space=pl.ANY),
            pl.BlockSpec((None, None, d, f2), by_expert),
            pl.BlockSpec((None, None, 1, f2), by_expert),
            pl.BlockSpec((None, None, f, d), by_expert),
            pl.BlockSpec((None, None, 1, d), by_expert),
        ],
        out_specs=pl.BlockSpec(memory_space=pl.ANY),
        scratch_shapes=[pltpu.VMEM((d, f2), BF16), pltpu.VMEM((f, d), BF16),
                        pltpu.VMEM((2, tm, d), F32), pltpu.VMEM((2, tm, d), F32),
                        pltpu.SemaphoreType.DMA((2,)), pltpu.SemaphoreType.DMA((2,))],
    )
    return pl.pallas_call(
        _ffn_kernel,
        grid_spec=gs,
        out_shape=jax.ShapeDtypeStruct((r, d), F32),
        compiler_params=_params(has_side_effects=True),
        name="moe_ffn",
    )(tile_start, tiles_e, xs, w_up, b_up[:, :, None, :], w_dn, b_dn[:, :, None, :])


def _combine_kernel(pc8_ref, loff_ref, gst_ref, used_ref, x_ref, lpos_ref, gate_ref, gfin_ref, ys_ref, o_ref,
                    buf_ref, sem, *, ne, final_norm):
    b = pl.program_id(0)
    nb = pl.num_programs(0)
    tb = x_ref.shape[0]
    lr = buf_ref.shape[1]

    def fetch(blk):
        def body(e, c):
            j = blk * ne + e
            _run_copies(pc8_ref[j], ys_ref, gst_ref[j], buf_ref.at[blk % 2], loff_ref[j], sem.at[blk % 2])
            return c
        lax.fori_loop(0, ne, body, 0)

    slot = b % 2

    @pl.when(b == 0)
    def _():
        buf_ref[...] = jnp.zeros(buf_ref.shape, F32)
        fetch(b)

    @pl.when(b + 1 < nb)
    def _():
        fetch(b + 1)

    lp = lpos_ref[0]
    gt = gate_ref[0]
    iota_r = lax.broadcasted_iota(jnp.int32, (lr, tb), 0)
    wperm = jnp.zeros((lr, tb), F32)
    for k in range(TOP_K):
        wperm = jnp.where(iota_r == lp[k:k + 1, :], gt[k:k + 1, :], wperm)

    _wait_rows(used_ref[b], ys_ref, buf_ref.at[slot], sem.at[slot])
    row = lax.broadcasted_iota(jnp.int32, (lr, 1), 0)
    ys_local = jnp.where(row < used_ref[b], buf_ref[slot], 0.0).astype(BF16)
    out = lax.dot_general(wperm.astype(BF16), ys_local, (((0,), (0,)), ((), ())), preferred_element_type=F32)
    y = x_ref[...] + out
    if final_norm:
        y = _rms(y, gfin_ref[...])
    o_ref[...] = y


def _combine_call(x, lpos, gate, gfin, ys, pc8, loff, gst, used, *, ne, final_norm):
    n, d = x.shape
    tb = TOKEN_BLOCK
    nb = n // tb
    lr = _local_rows(ne)
    gs = pltpu.PrefetchScalarGridSpec(
        num_scalar_prefetch=4,
        grid=(nb,),
        in_specs=[pl.BlockSpec((tb, d), lambda i, *_: (i, 0)),
                  pl.BlockSpec((1, TOP_K, tb), lambda i, *_: (i, 0, 0)),
                  pl.BlockSpec((1, TOP_K, tb), lambda i, *_: (i, 0, 0)),
                  pl.BlockSpec((1, d), lambda i, *_: (0, 0)),
                  pl.BlockSpec(memory_space=pl.ANY)],
        out_specs=pl.BlockSpec((tb, d), lambda i, *_: (i, 0)),
        scratch_shapes=[pltpu.VMEM((2, lr, d), F32), pltpu.SemaphoreType.DMA((2,))],
    )
    return pl.pallas_call(
        functools.partial(_combine_kernel, ne=ne, final_norm=final_norm),
        grid_spec=gs,
        out_shape=jax.ShapeDtypeStruct((n, d), F32),
        compiler_params=_params(),
        name="moe_combine",
    )(pc8, loff, gst, used, x, lpos, gate, gfin, ys)


def _moe_layer(x, g_ffn, w_r, b_r, w_up, b_up, w_dn, b_dn, g_final, *, layer, final_norm):
    n, d = x.shape
    ne = w_r.shape[1]
    tb, tm = TOKEN_BLOCK, FFN_TILE
    nb = n // tb
    gate, lpos, cnt = _route_call(x, g_ffn, w_r.T, b_r[:, None])
    cnt = cnt[:, :, 0]
    pc = (cnt + (SUBLANES - 1)) // SUBLANES * SUBLANES
    loff = jnp.cumsum(pc, axis=1) - pc
    rows_e = jnp.sum(pc, axis=0)
    tiles_e = (rows_e + tm - 1) // tm
    tile_end = jnp.cumsum(tiles_e)
    tile_start = tile_end - tiles_e
    gst = (tile_start * tm)[None, :] + jnp.cumsum(pc, axis=0) - pc
    max_rows = n * TOP_K + nb * ne * (SUBLANES - 1)
    max_tiles = (max_rows + ne * (tm - SUBLANES)) // tm
    i32 = lambda a: a.astype(jnp.int32).reshape(-1)
    pc8, loff, gst = i32(pc // SUBLANES), i32(loff), i32(gst)
    used = i32(jnp.sum(pc, axis=1))
    xs = _sort_call(x, g_ffn, lpos, pc8, loff, gst, used, n_rows_out=max_tiles * tm, ne=ne)
    ys = _ffn_call(xs, w_up, b_up, w_dn, b_dn, i32(tile_start), i32(tiles_e), layer=layer)
    return _combine_call(x, lpos, gate, g_final, ys, pc8, loff, gst, used, ne=ne, final_norm=final_norm)


def _time_major(a):
    b, l, d = a.shape
    return a.transpose(1, 0, 2).reshape(l * b, d)


def _batch_major(a, b):
    n, d = a.shape
    return a.reshape(n // b, b, d).transpose(1, 0, 2)


def kernel(x_prompt, x_sample, cache_pool, state_ssm_re, state_ssm_im, norm_mix_g, norm_ffn_g, norm_final_g,
           pool_w, pool_scale, ssm_a_re, ssm_a_im, ssm_log_dt, ssm_b_re, ssm_b_im, ssm_c_re, ssm_c_im,
           ssm_d, ssm_glu_w, ssm_glu_gate, router_w, router_b, moe_w_up, moe_b_up, moe_w_down, moe_b_down):
    bp, lp, d = x_prompt.shape
    bs, ls, _ = x_sample.shape
    depth = norm_mix_g.shape[0]
    assert depth == 2 and cache_pool.shape[0] == 1 and state_ssm_re.shape[0] == 1
    n_p, n_s = bp * lp, bs * ls
    row = lambda v: v.reshape(1, -1)

    x = jnp.concatenate([_time_major(x_prompt), _time_major(x_sample)], axis=0)

    pw = pool_w[0].astype(BF16)
    hist_p = jnp.zeros((POOL_HIST * bp, d), F32)
    hist_s = _time_major(cache_pool[0])
    x1, pool_p = _pool_call(x, None, hist_p, row(norm_mix_g[0]), pw, row(pool_scale[0]),
                            row0=0, n_rows=n_p, bt=bp, start_pos=0)
    x1, pool_s = _pool_call(x, x1, hist_s, row(norm_mix_g[0]), pw, row(pool_scale[0]),
                            row0=n_p, n_rows=n_s, bt=bs, start_pos=PAST_LEN)
    x2 = _moe_layer(x1, row(norm_ffn_g[0]), router_w[0], router_b[0], moe_w_up, moe_b_up, moe_w_down, moe_b_down,
                    row(norm_final_g), layer=0, final_norm=False)

    ar, ai, wb, wc = _ssm_weights(ssm_a_re[0], ssm_a_im[0], ssm_log_dt[0], ssm_b_re[0], ssm_b_im[0],
                                  ssm_c_re[0], ssm_c_im[0])
    wglu = jnp.concatenate([ssm_glu_w[0], ssm_glu_gate[0]], axis=1).astype(BF16)
    g_groups, p_state = ssm_a_re.shape[1:]
    sc = g_groups * p_state
    zero_state = jnp.zeros((bp, sc), F32)
    x3, re_p, im_p = _ssm_call(x2, None, row(norm_mix_g[1]), zero_state, zero_state, ar, ai, wb, wc,
                               row(ssm_d[0]), wglu, row0=0, n_rows=n_p, bt=bp)
    x3, re_s, im_s = _ssm_call(x2, x3, row(norm_mix_g[1]), state_ssm_re[0].reshape(bs, sc),
                               state_ssm_im[0].reshape(bs, sc), ar, ai, wb, wc,
                               row(ssm_d[0]), wglu, row0=n_p, n_rows=n_s, bt=bs)
    y = _moe_layer(x3, row(norm_ffn_g[1]), router_w[1], router_b[1], moe_w_up, moe_b_up, moe_w_down, moe_b_down,
                   row(norm_final_g), layer=1, final_norm=True)

    st = lambda a, b: a.reshape(1, b, g_groups, p_state)
    return (_batch_major(y[:n_p], bp), _batch_major(y[n_p:], bs),
            _batch_major(pool_p, bp)[None], _batch_major(pool_s, bs)[None],
            st(re_p, bp), st(im_p, bp), st(re_s, bs), st(im_s, bs))
```

```python
import functools
import math

import jax
import jax.numpy as jnp
from jax import lax
from jax.experimental import pallas as pl
from jax.experimental.pallas import tpu as pltpu

F32 = jnp.float32
BF16 = jnp.bfloat16

POOL_WINDOWS = (2, 4, 8, 16)
POOL_HIST = max(POOL_WINDOWS) - 1
SSM_GROUP_CH = 16
SSM_STATE = 64
TOP_K = 4
SWIGLU_LIMIT = 7.0
SWIGLU_ALPHA = 1.702
RMS_EPS = 1e-5
PAST_LEN = 1024

SUBLANES = 8
LANES = 128
MXU_DIM = 256
VMEM_LIMIT = 56 * 1024 * 1024

MIX_ROWS = 512
TOKEN_BLOCK = 256
FFN_TILE = 512
PERM_CHUNK = 128
SSM_SLAB = MXU_DIM


def _rms(x, g):
    return x * lax.rsqrt(jnp.mean(x * x, axis=-1, keepdims=True) + RMS_EPS) * g


def _params(**kw):
    return pltpu.CompilerParams(dimension_semantics=("arbitrary",), vmem_limit_bytes=VMEM_LIMIT, **kw)


def _pool_phase(chunk, x_ref, g_ref, w_ref, scale_ref, o_ref, hist_ref, hist_out_ref, ext_ref, *, bt, start_pos):
    rc = x_ref.shape[0]
    tc = rc // bt
    hr = POOL_HIST * bt
    gw = x_ref.shape[1] // len(POOL_WINDOWS)

    @pl.when(chunk == 0)
    def _():
        ext_ref[0:hr, :] = hist_ref[...]

    x = x_ref[...]
    u = _rms(x, g_ref[...])
    ext_ref[hr:hr + rc, :] = u
    row = lax.broadcasted_iota(jnp.int32, (rc, 1), 0)
    pos = start_pos + chunk * tc + row // bt
    outs = []
    for gi, w in enumerate(POOL_WINDOWS):
        c0 = gi * gw
        s = ext_ref[hr:hr + rc, c0:c0 + gw]
        for j in range(1, w):
            s = s + ext_ref[hr - j * bt:hr - j * bt + rc, c0:c0 + gw]
        cnt = jnp.minimum(w, pos + 1).astype(F32)
        mixed = s / cnt - u[:, c0:c0 + gw]
        outs.append(jnp.dot(mixed.astype(BF16), w_ref[gi], preferred_element_type=F32))
    y = jnp.concatenate(outs, axis=-1) * scale_ref[...]
    o_ref[...] = x + y
    ext_ref[0:hr, :] = ext_ref[rc:rc + hr, :]
    hist_out_ref[...] = ext_ref[0:hr, :]


def _pool_kernel(x_ref, hist_p_ref, hist_s_ref, g_ref, w_ref, scale_ref, o_ref, hout_p_ref, hout_s_ref,
                 ext_p_ref, ext_s_ref, *, n_chunks_p, bt_p, bt_s):
    i = pl.program_id(0)

    @pl.when(i < n_chunks_p)
    def _():
        _pool_phase(i, x_ref, g_ref, w_ref, scale_ref, o_ref, hist_p_ref, hout_p_ref, ext_p_ref,
                    bt=bt_p, start_pos=0)

    @pl.when(i >= n_chunks_p)
    def _():
        _pool_phase(i - n_chunks_p, x_ref, g_ref, w_ref, scale_ref, o_ref, hist_s_ref, hout_s_ref, ext_s_ref,
                    bt=bt_s, start_pos=PAST_LEN)


def _pool_call(x_all, hist_p, hist_s, g, w_bf, scale, *, n_p, bt_p, bt_s):
    n, d = x_all.shape
    rc = MIX_ROWS
    hr_p, hr_s = POOL_HIST * bt_p, POOL_HIST * bt_s
    assert n_p % rc == 0 and n % rc == 0 and rc > max(hr_p, hr_s) and rc % bt_p == 0 and rc % bt_s == 0
    kern = functools.partial(_pool_kernel, n_chunks_p=n_p // rc, bt_p=bt_p, bt_s=bt_s)
    const2 = lambda i: (0, 0)
    return pl.pallas_call(
        kern,
        grid=(n // rc,),
        in_specs=[
            pl.BlockSpec((rc, d), lambda i: (i, 0)),
            pl.BlockSpec((hr_p, d), const2),
            pl.BlockSpec((hr_s, d), const2),
            pl.BlockSpec((1, d), const2),
            pl.BlockSpec(w_bf.shape, lambda i: (0, 0, 0)),
            pl.BlockSpec((1, d), const2),
        ],
        out_specs=[pl.BlockSpec((rc, d), lambda i: (i, 0)),
                   pl.BlockSpec((hr_p, d), const2), pl.BlockSpec((hr_s, d), const2)],
        out_shape=[jax.ShapeDtypeStruct((n, d), F32),
                   jax.ShapeDtypeStruct((hr_p, d), F32), jax.ShapeDtypeStruct((hr_s, d), F32)],
        scratch_shapes=[pltpu.VMEM((hr_p + rc, d), F32), pltpu.VMEM((hr_s + rc, d), F32)],
        compiler_params=_params(),
        name="pool_mixer",
    )(x_all, hist_p, hist_s, g, w_bf, scale)


def _gelu_tanh(x):
    return 0.5 * x * (1.0 + jnp.tanh(math.sqrt(2.0 / math.pi) * (x + 0.044715 * (x * x * x))))


def _ssm_phase(chunk, x_ref, g_ref, ar_ref, ai_ref, wb_ref, wc_ref, dsk_ref, wglu_ref, o_ref, bu_ref, y_ref,
               h0r_ref, h0i_ref, hr_out_ref, hi_out_ref, hre_ref, him_ref):
    bt = hre_ref.shape[0]
    tc = x_ref.shape[0] // bt
    d = x_ref.shape[1]
    n_slab = d // SSM_SLAB
    sw = hre_ref.shape[1] // n_slab

    @pl.when(chunk == 0)
    def _():
        hre_ref[...] = h0r_ref[...]
        him_ref[...] = h0i_ref[...]

    x = x_ref[...]
    u = _rms(x, g_ref[...])
    ub = u.astype(BF16)
    for s in range(n_slab):
        bu_ref[...] = jnp.dot(ub[:, s * SSM_SLAB:(s + 1) * SSM_SLAB], wb_ref[s], preferred_element_type=F32)
        ar = ar_ref[s]
        ai = ai_ref[s]
        for hb in range(bt // SUBLANES):
            b0 = hb * SUBLANES

            def step(t, carry, b0=b0, ar=ar, ai=ai):
                h_re, h_im = carry
                r0 = pl.multiple_of(t * bt + b0, SUBLANES)
                n_re = ar * h_re - ai * h_im + bu_ref[pl.ds(r0, SUBLANES), 0:sw]
                n_im = ar * h_im + ai * h_re + bu_ref[pl.ds(r0, SUBLANES), sw:2 * sw]
                bu_ref[pl.ds(r0, SUBLANES), 0:sw] = n_re
                bu_ref[pl.ds(r0, SUBLANES), sw:2 * sw] = n_im
                return n_re, n_im

            h_re, h_im = lax.fori_loop(
                0, tc, step,
                (hre_ref[b0:b0 + SUBLANES, s * sw:(s + 1) * sw], him_ref[b0:b0 + SUBLANES, s * sw:(s + 1) * sw]),
                unroll=4)
            hre_ref[b0:b0 + SUBLANES, s * sw:(s + 1) * sw] = h_re
            him_ref[b0:b0 + SUBLANES, s * sw:(s + 1) * sw] = h_im
        y_ref[:, s * SSM_SLAB:(s + 1) * SSM_SLAB] = jnp.dot(
            bu_ref[...].astype(BF16), wc_ref[s], preferred_element_type=F32)
    y = y_ref[...] + dsk_ref[...] * u
    gl = _gelu_tanh(y).astype(BF16)
    z = jnp.dot(gl, wglu_ref[...], preferred_element_type=F32)
    o_ref[...] = x + z[:, :d] * jax.nn.sigmoid(z[:, d:])
    hr_out_ref[...] = hre_ref[...]
    hi_out_ref[...] = him_ref[...]


def _ssm_kernel(x_ref, g_ref, ar_ref, ai_ref, wb_ref, wc_ref, dsk_ref, wglu_ref,
                h0r_p_ref, h0i_p_ref, h0r_s_ref, h0i_s_ref,
                o_ref, hr_p_ref, hi_p_ref, hr_s_ref, hi_s_ref,
                bu_ref, y_ref, hre_p_ref, him_p_ref, hre_s_ref, him_s_ref, *, n_chunks_p):
    i = pl.program_id(0)
    shared = (x_ref, g_ref, ar_ref, ai_ref, wb_ref, wc_ref, dsk_ref, wglu_ref, o_ref, bu_ref, y_ref)

    @pl.when(i < n_chunks_p)
    def _():
        _ssm_phase(i, *shared, h0r_p_ref, h0i_p_ref, hr_p_ref, hi_p_ref, hre_p_ref, him_p_ref)

    @pl.when(i >= n_chunks_p)
    def _():
        _ssm_phase(i - n_chunks_p, *shared, h0r_s_ref, h0i_s_ref, hr_s_ref, hi_s_ref, hre_s_ref, him_s_ref)


def _ssm_call(x_all, g, ar, ai, wb, wc, dsk, wglu, h0r_p, h0i_p, h0r_s, h0i_s, *, n_p):
    n, d = x_all.shape
    rc = MIX_ROWS
    bt_p, sc = h0r_p.shape
    bt_s = h0r_s.shape[0]
    assert n_p % rc == 0 and n % rc == 0 and bt_p % SUBLANES == 0 and bt_s % SUBLANES == 0
    assert rc % bt_p == 0 and rc % bt_s == 0
    n_slab = d // SSM_SLAB
    const2 = lambda i: (0, 0)
    const3 = lambda i: (0, 0, 0)
    state = lambda bt: pl.BlockSpec((bt, sc), const2)
    state_shape = lambda bt: jax.ShapeDtypeStruct((bt, sc), F32)
    return pl.pallas_call(
        functools.partial(_ssm_kernel, n_chunks_p=n_p // rc),
        grid=(n // rc,),
        in_specs=[
            pl.BlockSpec((rc, d), lambda i: (i, 0)),
            pl.BlockSpec((1, d), const2),
            pl.BlockSpec(ar.shape, const3),
            pl.BlockSpec(ai.shape, const3),
            pl.BlockSpec(wb.shape, const3, pipeline_mode=pl.Buffered(1)),
            pl.BlockSpec(wc.shape, const3, pipeline_mode=pl.Buffered(1)),
            pl.BlockSpec((1, d), const2),
            pl.BlockSpec(wglu.shape, const2, pipeline_mode=pl.Buffered(1)),
            state(bt_p), state(bt_p), state(bt_s), state(bt_s),
        ],
        out_specs=[pl.BlockSpec((rc, d), lambda i: (i, 0)), state(bt_p), state(bt_p), state(bt_s), state(bt_s)],
        out_shape=[jax.ShapeDtypeStruct((n, d), F32),
                   state_shape(bt_p), state_shape(bt_p), state_shape(bt_s), state_shape(bt_s)],
        scratch_shapes=[pltpu.VMEM((rc, 2 * sc // n_slab), F32), pltpu.VMEM((rc, d), F32),
                        pltpu.VMEM((bt_p, sc), F32), pltpu.VMEM((bt_p, sc), F32),
                        pltpu.VMEM((bt_s, sc), F32), pltpu.VMEM((bt_s, sc), F32)],
        compiler_params=_params(),
        name="s5_mixer",
    )(x_all, g, ar, ai, wb, wc, dsk, wglu, h0r_p, h0i_p, h0r_s, h0i_s)


def _ssm_weights(a_re, a_im, log_dt, b_re, b_im, c_re, c_im):
    g, p = a_re.shape
    h = b_re.shape[2]
    gs = SSM_SLAB // h
    n_slab = g // gs
    dt = jnp.exp(log_dt)[:, None]
    mag = jnp.exp(a_re * dt)
    abr = mag * jnp.cos(a_im * dt)
    abi = mag * jnp.sin(a_im * dt)
    den = a_re * a_re + a_im * a_im
    qr = ((abr - 1.0) * a_re + abi * a_im) / den
    qi = (abi * a_re - (abr - 1.0) * a_im) / den
    bbr = qr[..., None] * b_re - qi[..., None] * b_im
    bbi = qr[..., None] * b_im + qi[..., None] * b_re
    eye = jnp.eye(gs, dtype=F32)

    def in_blockdiag(m):
        m = m.reshape(n_slab, gs, p, h).transpose(0, 1, 3, 2)
        return jnp.einsum("ab,sahp->sahbp", eye, m).reshape(n_slab, gs * h, gs * p)

    def out_blockdiag(m):
        m = m.reshape(n_slab, gs, h, p).transpose(0, 1, 3, 2)
        return jnp.einsum("ab,saph->sapbh", eye, m).reshape(n_slab, gs * p, gs * h)

    wb = jnp.concatenate([in_blockdiag(bbr), in_blockdiag(bbi)], axis=-1).astype(BF16)
    wc = jnp.concatenate([out_blockdiag(c_re), out_blockdiag(-c_im)], axis=1).astype(BF16)
    ar = jnp.broadcast_to(abr.reshape(n_slab, 1, gs * p), (n_slab, SUBLANES, gs * p))
    ai = jnp.broadcast_to(abi.reshape(n_slab, 1, gs * p), (n_slab, SUBLANES, gs * p))
    return ar, ai, wb, wc


def _route_kernel(x_ref, g_ref, wr_ref, br_ref, gate_ref, lpos_ref, cnt_ref):
    tb = x_ref.shape[0]
    ne = wr_ref.shape[0]
    u = _rms(x_ref[...], g_ref[...])
    logits = lax.dot_general(wr_ref[...], u, (((1,), (1,)), ((), ())),
                             precision=lax.Precision.HIGHEST, preferred_element_type=F32) + br_ref[...]
    iota_e = lax.broadcasted_iota(jnp.int32, (ne, tb), 0)
    vals, ids = [], []
    l = logits
    for _ in range(TOP_K):
        m = jnp.max(l, axis=0, keepdims=True)
        idx = jnp.min(jnp.where(l == m, iota_e, ne), axis=0, keepdims=True)
        vals.append(m)
        ids.append(idx)
        l = jnp.where(iota_e == idx, -jnp.inf, l)
    ex = [jnp.exp(v - vals[0]) for v in vals]
    den = ex[0] + ex[1] + ex[2] + ex[3]
    gate_ref[0] = jnp.concatenate([e / den for e in ex], axis=0)

    member = jnp.zeros((ne, tb), F32)
    for idx in ids:
        member = jnp.where(iota_e == idx, 1.0, member)
    tri = (lax.broadcasted_iota(jnp.int32, (tb, tb), 0) <= lax.broadcasted_iota(jnp.int32, (tb, tb), 1))
    csum = jnp.dot(member.astype(BF16), tri.astype(BF16), preferred_element_type=F32)
    cnt = csum[:, tb - 1:tb]
    pcnt = jnp.floor((cnt + (SUBLANES - 1)) / SUBLANES) * SUBLANES
    ltri = (lax.broadcasted_iota(jnp.int32, (ne, ne), 1) < lax.broadcasted_iota(jnp.int32, (ne, ne), 0))
    loff = jnp.dot(ltri.astype(BF16), jnp.broadcast_to(pcnt, (ne, LANES)).astype(BF16),
                   preferred_element_type=F32)[:, 0:1]
    lpos = []
    for idx in ids:
        sel = iota_e == idx
        lpos.append(jnp.sum(jnp.where(sel, csum - 1.0 + loff, 0.0), axis=0, keepdims=True))
    lpos_ref[0] = jnp.concatenate(lpos, axis=0).astype(jnp.int32)
    cnt_ref[0] = jnp.broadcast_to(cnt, (ne, LANES)).astype(jnp.int32)


def _route_call(x, g, wr_t, br):
    n, d = x.shape
    tb = TOKEN_BLOCK
    nb = n // tb
    ne = wr_t.shape[0]
    const2 = lambda i: (0, 0)
    return pl.pallas_call(
        _route_kernel,
        grid=(nb,),
        in_specs=[pl.BlockSpec((tb, d), lambda i: (i, 0)), pl.BlockSpec((1, d), const2),
                  pl.BlockSpec((ne, d), const2), pl.BlockSpec((ne, 1), const2)],
        out_specs=[pl.BlockSpec((1, TOP_K, tb), lambda i: (i, 0, 0)),
                   pl.BlockSpec((1, TOP_K, tb), lambda i: (i, 0, 0)),
                   pl.BlockSpec((1, ne, LANES), lambda i: (i, 0, 0))],
        out_shape=[jax.ShapeDtypeStruct((nb, TOP_K, tb), F32),
                   jax.ShapeDtypeStruct((nb, TOP_K, tb), jnp.int32),
                   jax.ShapeDtypeStruct((nb, ne, LANES), jnp.int32)],
        compiler_params=_params(),
        name="moe_route",
    )(x, g, wr_t, br)


def _local_rows(ne):
    worst = TOKEN_BLOCK * TOP_K + ne * (SUBLANES - 1)
    return -(-worst // LANES) * LANES


def _run_copies(n8, src_ref, src0, dst_ref, dst0, sem):
    top = (TOKEN_BLOCK // SUBLANES).bit_length() - 1
    for bit in range(top, -1, -1):
        size = SUBLANES << bit
        off = (n8 >> (bit + 1)) << (bit + 1 + 3)

        @pl.when(((n8 >> bit) & 1) == 1)
        def _(size=size, off=off):
            pltpu.make_async_copy(
                src_ref.at[pl.ds(pl.multiple_of(src0 + off, SUBLANES), size)],
                dst_ref.at[pl.ds(pl.multiple_of(dst0 + off, SUBLANES), size)], sem).start()


def _wait_rows(n_rows, src_ref, dst_ref, sem):
    n8 = n_rows // SUBLANES
    top = (min(src_ref.shape[0], dst_ref.shape[0]) // SUBLANES).bit_length() - 1
    for bit in range(top, -1, -1):
        size = SUBLANES << bit

        @pl.when(((n8 >> bit) & 1) == 1)
        def _(size=size):
            pltpu.make_async_copy(src_ref.at[pl.ds(0, size)], dst_ref.at[pl.ds(0, size)], sem).wait()


def _onehot_chunk(r0, tb, lp, vals):
    iota_r = lax.broadcasted_iota(jnp.int32, (PERM_CHUNK, tb), 0) + r0
    chunk = jnp.zeros((PERM_CHUNK, tb), F32)
    for k in range(TOP_K):
        chunk = jnp.where(iota_r == lp[k:k + 1, :], vals[k], chunk)
    return chunk.astype(BF16)


def _sort_kernel(pc8_ref, loff_ref, gst_ref, used_ref, gap8_ref, gapst_ref, x_ref, g_ref, lpos_ref, xs_ref,
                 buf0_ref, buf1_ref, zero_ref, sem, zsem, *, ne):
    b = pl.program_id(0)
    nb = pl.num_programs(0)
    tb = x_ref.shape[0]
    bufs = (buf0_ref, buf1_ref)
    lr = buf0_ref.shape[0]

    def drain(blk, live, p):
        _wait_rows(jnp.where(live, used_ref[blk], 0), bufs[p], xs_ref, sem.at[p])

    def start_runs(blk, live, p):
        for e in range(ne):
            j = blk * ne + e
            _run_copies(jnp.where(live, pc8_ref[j], 0), bufs[p], loff_ref[j], xs_ref, gst_ref[j], sem.at[p])

    @pl.when(b == 0)
    def _():
        zero_ref[...] = jnp.zeros(zero_ref.shape, F32)

        def fill(e, c):
            _run_copies(gap8_ref[e], zero_ref, 0, xs_ref, gapst_ref[e], zsem)
            return c

        lax.fori_loop(0, ne, fill, 0)

    def step(p):
        drain(jnp.maximum(b - 2, 0), b >= 2, p)
        start_runs(jnp.maximum(b - 1, 0), b >= 1, 1 - p)

        ub = _rms(x_ref[...], g_ref[...]).astype(BF16)
        lp = lpos_ref[0]
        for r0 in range(0, lr, PERM_CHUNK):
            bufs[p][r0:r0 + PERM_CHUNK, :] = jnp.dot(
                _onehot_chunk(r0, tb, lp, [1.0] * TOP_K), ub, preferred_element_type=F32)

        @pl.when(b == nb - 1)
        def _():
            start_runs(b, True, p)
            drain(jnp.maximum(b - 1, 0), b >= 1, 1 - p)
            drain(b, True, p)

            def fill_done(e, c):
                _wait_rows(gap8_ref[e] * SUBLANES, zero_ref, xs_ref, zsem)
                return c

            lax.fori_loop(0, ne, fill_done, 0)

    for p in range(2):
        pl.when(b % 2 == p)(functools.partial(step, p))


def _sort_call(x, g, lpos, pc8, loff, gst, used, gap8, gapst, *, n_rows_out, ne):
    n, d = x.shape
    tb = TOKEN_BLOCK
    nb = n // tb
    lr = _local_rows(ne)
    gs = pltpu.PrefetchScalarGridSpec(
        num_scalar_prefetch=6,
        grid=(nb,),
        in_specs=[pl.BlockSpec((tb, d), lambda i, *_: (i, 0)), pl.BlockSpec((1, d), lambda i, *_: (0, 0)),
                  pl.BlockSpec((1, TOP_K, tb), lambda i, *_: (i, 0, 0))],
        out_specs=pl.BlockSpec(memory_space=pl.ANY),
        scratch_shapes=[pltpu.VMEM((lr, d), F32), pltpu.VMEM((lr, d), F32), pltpu.VMEM((TOKEN_BLOCK, d), F32),
                        pltpu.SemaphoreType.DMA((2,)), pltpu.SemaphoreType.DMA(())],
    )
    return pl.pallas_call(
        functools.partial(_sort_kernel, ne=ne),
        grid_spec=gs,
        out_shape=jax.ShapeDtypeStruct((n_rows_out, d), F32),
        compiler_params=_params(has_side_effects=True),
        name="moe_sort",
    )(pc8, loff, gst, used, gap8, gapst, x, g, lpos)


def _ffn_kernel(t0_ref, nt_ref, half_ref, xs_ref, wup_ref, bup_ref, wdn_ref, bdn_ref, ys_ref,
                wup_bf, wdn_bf, xbuf, ybuf, sem_in, sem_out):
    e = pl.program_id(0)
    ne = pl.num_programs(0)
    tm = xbuf.shape[1]
    f = wdn_ref.shape[0]
    t0 = t0_ref[e]
    nt = nt_ref[e]
    total = t0_ref[ne - 1] + nt_ref[ne - 1]

    def fetch(g, rows):
        return pltpu.make_async_copy(xs_ref.at[pl.ds(pl.multiple_of(g * tm, tm), rows)],
                                     xbuf.at[g % 2, pl.ds(0, rows)], sem_in.at[g % 2])

    def writeback(g, rows):
        return pltpu.make_async_copy(ybuf.at[g % 2, pl.ds(0, rows)],
                                     ys_ref.at[pl.ds(pl.multiple_of(g * tm, tm), rows)], sem_out.at[g % 2])

    def by_size(g, fn):
        @pl.when(half_ref[g] == 0)
        def _():
            fn(tm)

        @pl.when(half_ref[g] != 0)
        def _():
            fn(tm // 2)

    @pl.when(e == 0)
    def _():
        by_size(0, lambda rows: fetch(0, rows).start())

    @pl.when(nt > 0)
    def _():
        wup_bf[...] = wup_ref[...].astype(BF16)
        wdn_bf[...] = wdn_ref[...].astype(BF16)

    def tile(j, c):
        g = t0 + j

        def run(rows):
            fetch(g, rows).wait()

            @pl.when(g + 1 < total)
            def _():
                by_size(g + 1, lambda r: fetch(g + 1, r).start())

            @pl.when(g >= 2)
            def _():
                by_size(g - 2, lambda r: writeback(g - 2, r).wait())

            x = xbuf[g % 2, pl.ds(0, rows)].astype(BF16)
            gu = jnp.dot(x, wup_bf[...], preferred_element_type=F32) + bup_ref[...]
            glu = jnp.minimum(gu[:, :f], SWIGLU_LIMIT)
            lin = jnp.clip(gu[:, f:], -SWIGLU_LIMIT, SWIGLU_LIMIT)
            act = glu * jax.nn.sigmoid(SWIGLU_ALPHA * glu) * (lin + 1.0)
            ybuf[g % 2, pl.ds(0, rows)] = (
                jnp.dot(act.astype(BF16), wdn_bf[...], preferred_element_type=F32) + bdn_ref[...])
            writeback(g, rows).start()

        by_size(g, run)
        return c

    lax.fori_loop(0, nt, tile, 0)

    @pl.when(e == ne - 1)
    def _():
        @pl.when(total >= 2)
        def _():
            by_size(total - 2, lambda r: writeback(total - 2, r).wait())

        by_size(total - 1, lambda r: writeback(total - 1, r).wait())


def _ffn_call(xs, w_up, b_up, w_dn, b_dn, tile_start, tiles_e, tile_half, *, layer):
    r, d = xs.shape
    tm = FFN_TILE
    _, ne, _, f2 = w_up.shape
    f = w_dn.shape[2]
    by_expert = lambda e, *_: (layer, e, 0, 0)
    gs = pltpu.PrefetchScalarGridSpec(
        num_scalar_prefetch=3,
        grid=(ne,),
        in_specs=[
            pl.BlockSpec(memory_space=pl.ANY),
            pl.BlockSpec((None, None, d, f2), by_expert),
            pl.BlockSpec((None, None, 1, f2), by_expert),
            pl.BlockSpec((None, None, f, d), by_expert),
            pl.BlockSpec((None, None, 1, d), by_expert),
        ],
        out_specs=pl.BlockSpec(memory_space=pl.ANY),
        scratch_shapes=[pltpu.VMEM((d, f2), BF16), pltpu.VMEM((f, d), BF16),
                        pltpu.VMEM((2, tm, d), F32), pltpu.VMEM((2, tm, d), F32),
                        pltpu.SemaphoreType.DMA((2,)), pltpu.SemaphoreType.DMA((2,))],
    )
    return pl.pallas_call(
        _ffn_kernel,
        grid_spec=gs,
        out_shape=jax.ShapeDtypeStruct((r, d), F32),
        compiler_params=_params(has_side_effects=True),
        name="moe_ffn",
    )(tile_start, tiles_e, tile_half, xs, w_up, b_up[:, :, None, :], w_dn, b_dn[:, :, None, :])


def _combine_kernel(pc8_ref, loff_ref, gst_ref, used_ref, x_ref, lpos_ref, gate_ref, gfin_ref, ys_ref, o_ref,
                    buf0_ref, buf1_ref, perm_ref, sem, *, ne, final_norm):
    b = pl.program_id(0)
    nb = pl.num_programs(0)
    tb = x_ref.shape[0]
    bufs = (buf0_ref, buf1_ref)
    lr = buf0_ref.shape[0]

    def fetch(blk, live, p):
        for e in range(ne):
            j = blk * ne + e
            _run_copies(jnp.where(live, pc8_ref[j], 0), ys_ref, gst_ref[j], bufs[p], loff_ref[j], sem.at[p])

    @pl.when(b == 0)
    def _():
        for buf in bufs:
            buf[...] = jnp.zeros(buf.shape, F32)
        fetch(b, True, 0)

    def step(p):
        _wait_rows(used_ref[b], ys_ref, bufs[p], sem.at[p])
        fetch(jnp.minimum(b + 1, nb - 1), b + 1 < nb, 1 - p)

        gt = gate_ref[0]
        lp = lpos_ref[0]
        for r0 in range(0, lr, PERM_CHUNK):
            perm_ref[r0:r0 + PERM_CHUNK, :] = _onehot_chunk(r0, tb, lp, [gt[k:k + 1, :] for k in range(TOP_K)])

        row = lax.broadcasted_iota(jnp.int32, (lr, 1), 0)
        ys_local = jnp.where(row < used_ref[b], bufs[p][...], 0.0).astype(BF16)
        out = lax.dot_general(perm_ref[...], ys_local, (((0,), (0,)), ((), ())), preferred_element_type=F32)
        y = x_ref[...] + out
        if final_norm:
            y = _rms(y, gfin_ref[...])
        o_ref[...] = y

    for p in range(2):
        pl.when(b % 2 == p)(functools.partial(step, p))


def _combine_call(x, lpos, gate, gfin, ys, pc8, loff, gst, used, *, ne, final_norm):
    n, d = x.shape
    tb = TOKEN_BLOCK
    nb = n // tb
    lr = _local_rows(ne)
    gs = pltpu.PrefetchScalarGridSpec(
        num_scalar_prefetch=4,
        grid=(nb,),
        in_specs=[pl.BlockSpec((tb, d), lambda i, *_: (i, 0)),
                  pl.BlockSpec((1, TOP_K, tb), lambda i, *_: (i, 0, 0)),
                  pl.BlockSpec((1, TOP_K, tb), lambda i, *_: (i, 0, 0)),
                  pl.BlockSpec((1, d), lambda i, *_: (0, 0)),
                  pl.BlockSpec(memory_space=pl.ANY)],
        out_specs=pl.BlockSpec((tb, d), lambda i, *_: (i, 0)),
        scratch_shapes=[pltpu.VMEM((lr, d), F32), pltpu.VMEM((lr, d), F32), pltpu.VMEM((lr, tb), BF16),
                        pltpu.SemaphoreType.DMA((2,))],
    )
    return pl.pallas_call(
        functools.partial(_combine_kernel, ne=ne, final_norm=final_norm),
        grid_spec=gs,
        out_shape=jax.ShapeDtypeStruct((n, d), F32),
        compiler_params=_params(),
        name="moe_combine",
    )(pc8, loff, gst, used, x, lpos, gate, gfin, ys)


def _moe_layer(x, g_ffn, w_r, b_r, w_up, b_up, w_dn, b_dn, g_final, *, layer, final_norm):
    n, d = x.shape
    ne = w_r.shape[1]
    tb, tm = TOKEN_BLOCK, FFN_TILE
    nb = n // tb
    gate, lpos, cnt = _route_call(x, g_ffn, w_r.T, b_r[:, None])
    cnt = cnt[:, :, 0]
    pc = (cnt + (SUBLANES - 1)) // SUBLANES * SUBLANES
    loff = jnp.cumsum(pc, axis=1) - pc
    rows_e = jnp.sum(pc, axis=0)
    tiles_e = (rows_e + tm - 1) // tm
    tile_end = jnp.cumsum(tiles_e)
    tile_start = tile_end - tiles_e
    gst = (tile_start * tm)[None, :] + jnp.cumsum(pc, axis=0) - pc
    tail = rows_e % tm
    half_e = (tail > 0) & (tail <= tm // 2)
    gap = tiles_e * tm - jnp.where(half_e, tm // 2, 0) - rows_e
    max_rows = n * TOP_K + nb * ne * (SUBLANES - 1)
    max_tiles = (max_rows + ne * (tm - SUBLANES)) // tm
    t = jnp.arange(max_tiles, dtype=jnp.int32)
    tile_half = jnp.sum((t[:, None] == (tile_end - 1)[None, :]) & half_e[None, :], axis=1)
    i32 = lambda a: a.astype(jnp.int32).reshape(-1)
    pc8, loff, gst = i32(pc // SUBLANES), i32(loff), i32(gst)
    used = i32(jnp.sum(pc, axis=1))
    xs = _sort_call(x, g_ffn, lpos, pc8, loff, gst, used, i32(gap // SUBLANES), i32(tile_start * tm + rows_e),
                    n_rows_out=max_tiles * tm, ne=ne)
    ys = _ffn_call(xs, w_up, b_up, w_dn, b_dn, i32(tile_start), i32(tiles_e), i32(tile_half), layer=layer)
    return _combine_call(x, lpos, gate, g_final, ys, pc8, loff, gst, used, ne=ne, final_norm=final_norm)


def _time_major(a):
    b, l, d = a.shape
    return a.transpose(1, 0, 2).reshape(l * b, d)


def _batch_major(a, b):
    n, d = a.shape
    return a.reshape(n // b, b, d).transpose(1, 0, 2)


def kernel(x_prompt, x_sample, cache_pool, state_ssm_re, state_ssm_im, norm_mix_g, norm_ffn_g, norm_final_g,
           pool_w, pool_scale, ssm_a_re, ssm_a_im, ssm_log_dt, ssm_b_re, ssm_b_im, ssm_c_re, ssm_c_im,
           ssm_d, ssm_glu_w, ssm_glu_gate, router_w, router_b, moe_w_up, moe_b_up, moe_w_down, moe_b_down):
    bp, lp, d = x_prompt.shape
    bs, ls, _ = x_sample.shape
    depth = norm_mix_g.shape[0]
    assert depth == 2 and cache_pool.shape[0] == 1 and state_ssm_re.shape[0] == 1
    n_p, n_s = bp * lp, bs * ls
    row = lambda v: v.reshape(1, -1)

    x = jnp.concatenate([_time_major(x_prompt), _time_major(x_sample)], axis=0)

    pw = pool_w[0].astype(BF16)
    hist_p = jnp.zeros((POOL_HIST * bp, d), F32)
    hist_s = _time_major(cache_pool[0])
    x1, pool_p, pool_s = _pool_call(x, hist_p, hist_s, row(norm_mix_g[0]), pw, row(pool_scale[0]),
                                    n_p=n_p, bt_p=bp, bt_s=bs)
    x2 = _moe_layer(x1, row(norm_ffn_g[0]), router_w[0], router_b[0], moe_w_up, moe_b_up, moe_w_down, moe_b_down,
                    row(norm_final_g), layer=0, final_norm=False)

    ar, ai, wb, wc = _ssm_weights(ssm_a_re[0], ssm_a_im[0], ssm_log_dt[0], ssm_b_re[0], ssm_b_im[0],
                                  ssm_c_re[0], ssm_c_im[0])
    wglu = jnp.concatenate([ssm_glu_w[0], ssm_glu_gate[0]], axis=1).astype(BF16)
    g_groups, p_state = ssm_a_re.shape[1:]
    sc = g_groups * p_state
    zero_state = jnp.zeros((bp, sc), F32)
    x3, re_p, im_p, re_s, im_s = _ssm_call(
        x2, row(norm_mix_g[1]), ar, ai, wb, wc, row(ssm_d[0]), wglu, zero_state, zero_state,
        state_ssm_re[0].reshape(bs, sc), state_ssm_im[0].reshape(bs, sc), n_p=n_p)
    y = _moe_layer(x3, row(norm_ffn_g[1]), router_w[1], router_b[1], moe_w_up, moe_b_up, moe_w_down, moe_b_down,
                   row(norm_final_g), layer=1, final_norm=True)

    st = lambda a, b: a.reshape(1, b, g_groups, p_state)
    return (_batch_major(y[:n_p], bp), _batch_major(y[n_p:], bs),
            _batch_major(pool_p, bp)[None], _batch_major(pool_s, bs)[None],
            st(re_p, bp), st(im_p, bp), st(re_s, bs), st(im_s, bs))
```

```python
import functools
import math

import jax
import jax.numpy as jnp
from jax import lax
from jax.experimental import pallas as pl
from jax.experimental.pallas import tpu as pltpu

F32 = jnp.float32
BF16 = jnp.bfloat16

POOL_WINDOWS = (2, 4, 8, 16)
POOL_HIST = max(POOL_WINDOWS) - 1
SSM_GROUP_CH = 16
SSM_STATE = 64
TOP_K = 4
SWIGLU_LIMIT = 7.0
SWIGLU_ALPHA = 1.702
RMS_EPS = 1e-5
PAST_LEN = 1024

SUBLANES = 8
LANES = 128
MXU_DIM = 256
VMEM_LIMIT = 56 * 1024 * 1024

MIX_ROWS = 512
TOKEN_BLOCK = 256
FFN_TILE = 512
PERM_CHUNK = 128
SSM_SLAB = MXU_DIM


def _rms(x, g):
    return x * lax.rsqrt(jnp.mean(x * x, axis=-1, keepdims=True) + RMS_EPS) * g


def _params(**kw):
    return pltpu.CompilerParams(dimension_semantics=("arbitrary",), vmem_limit_bytes=VMEM_LIMIT, **kw)


def _pool_phase(chunk, x_ref, g_ref, w_ref, scale_ref, o_ref, hist_ref, hist_out_ref, ext_ref, route,
                *, bt, start_pos):
    rc = x_ref.shape[0]
    tc = rc // bt
    hr = POOL_HIST * bt
    gw = x_ref.shape[1] // len(POOL_WINDOWS)

    @pl.when(chunk == 0)
    def _():
        ext_ref[0:hr, :] = hist_ref[...]

    x = x_ref[...]
    u = _rms(x, g_ref[...])
    ext_ref[hr:hr + rc, :] = u
    row = lax.broadcasted_iota(jnp.int32, (rc, 1), 0)
    pos = start_pos + chunk * tc + row // bt
    outs = []
    for gi, w in enumerate(POOL_WINDOWS):
        c0 = gi * gw
        s = ext_ref[hr:hr + rc, c0:c0 + gw]
        for j in range(1, w):
            s = s + ext_ref[hr - j * bt:hr - j * bt + rc, c0:c0 + gw]
        cnt = jnp.minimum(w, pos + 1).astype(F32)
        mixed = s / cnt - u[:, c0:c0 + gw]
        outs.append(jnp.dot(mixed.astype(BF16), w_ref[gi], preferred_element_type=F32))
    y = jnp.concatenate(outs, axis=-1) * scale_ref[...]
    out = x + y
    o_ref[...] = out
    _route_rows(out, *route)
    ext_ref[0:hr, :] = ext_ref[rc:rc + hr, :]
    hist_out_ref[...] = ext_ref[0:hr, :]


def _pool_kernel(x_ref, hist_p_ref, hist_s_ref, g_ref, w_ref, scale_ref, gffn_ref, wr_ref, br_ref,
                 o_ref, hout_p_ref, hout_s_ref, gate_ref, lpos_ref, cnt_ref,
                 ext_p_ref, ext_s_ref, *, n_chunks_p, bt_p, bt_s):
    i = pl.program_id(0)
    route = (gffn_ref, wr_ref, br_ref, gate_ref, lpos_ref, cnt_ref)

    @pl.when(i < n_chunks_p)
    def _():
        _pool_phase(i, x_ref, g_ref, w_ref, scale_ref, o_ref, hist_p_ref, hout_p_ref, ext_p_ref, route,
                    bt=bt_p, start_pos=0)

    @pl.when(i >= n_chunks_p)
    def _():
        _pool_phase(i - n_chunks_p, x_ref, g_ref, w_ref, scale_ref, o_ref, hist_s_ref, hout_s_ref, ext_s_ref,
                    route, bt=bt_s, start_pos=PAST_LEN)


def _pool_call(x_all, hist_p, hist_s, g, w_bf, scale, route_w, *, n_p, bt_p, bt_s):
    n, d = x_all.shape
    rc = MIX_ROWS
    hr_p, hr_s = POOL_HIST * bt_p, POOL_HIST * bt_s
    assert n_p % rc == 0 and n % rc == 0 and rc > max(hr_p, hr_s) and rc % bt_p == 0 and rc % bt_s == 0
    kern = functools.partial(_pool_kernel, n_chunks_p=n_p // rc, bt_p=bt_p, bt_s=bt_s)
    const2 = lambda i: (0, 0)
    r_in, r_out, r_shape = _route_specs(n, route_w)
    return pl.pallas_call(
        kern,
        grid=(n // rc,),
        in_specs=[
            pl.BlockSpec((rc, d), lambda i: (i, 0)),
            pl.BlockSpec((hr_p, d), const2),
            pl.BlockSpec((hr_s, d), const2),
            pl.BlockSpec((1, d), const2),
            pl.BlockSpec(w_bf.shape, lambda i: (0, 0, 0)),
            pl.BlockSpec((1, d), const2),
        ] + r_in,
        out_specs=[pl.BlockSpec((rc, d), lambda i: (i, 0)),
                   pl.BlockSpec((hr_p, d), const2), pl.BlockSpec((hr_s, d), const2)] + r_out,
        out_shape=[jax.ShapeDtypeStruct((n, d), F32),
                   jax.ShapeDtypeStruct((hr_p, d), F32), jax.ShapeDtypeStruct((hr_s, d), F32)] + r_shape,
        scratch_shapes=[pltpu.VMEM((hr_p + rc, d), F32), pltpu.VMEM((hr_s + rc, d), F32)],
        compiler_params=_params(),
        name="pool_mixer",
    )(x_all, hist_p, hist_s, g, w_bf, scale, *route_w)


def _gelu_tanh(x):
    return 0.5 * x * (1.0 + jnp.tanh(math.sqrt(2.0 / math.pi) * (x + 0.044715 * (x * x * x))))


def _ssm_phase(chunk, x_ref, g_ref, ar_ref, ai_ref, wb_ref, wc_ref, dsk_ref, wglu_ref, o_ref, bu_ref, y_ref, route,
               h0r_ref, h0i_ref, hr_out_ref, hi_out_ref, hre_ref, him_ref):
    bt = hre_ref.shape[0]
    tc = x_ref.shape[0] // bt
    d = x_ref.shape[1]
    n_slab = d // SSM_SLAB
    sw = hre_ref.shape[1] // n_slab

    @pl.when(chunk == 0)
    def _():
        hre_ref[...] = h0r_ref[...]
        him_ref[...] = h0i_ref[...]

    x = x_ref[...]
    u = _rms(x, g_ref[...])
    ub = u.astype(BF16)
    for s in range(n_slab):
        bu_ref[...] = jnp.dot(ub[:, s * SSM_SLAB:(s + 1) * SSM_SLAB], wb_ref[s], preferred_element_type=F32)
        ar = ar_ref[s]
        ai = ai_ref[s]
        for hb in range(bt // SUBLANES):
            b0 = hb * SUBLANES

            def step(t, carry, b0=b0, ar=ar, ai=ai):
                h_re, h_im = carry
                r0 = pl.multiple_of(t * bt + b0, SUBLANES)
                n_re = ar * h_re - ai * h_im + bu_ref[pl.ds(r0, SUBLANES), 0:sw]
                n_im = ar * h_im + ai * h_re + bu_ref[pl.ds(r0, SUBLANES), sw:2 * sw]
                bu_ref[pl.ds(r0, SUBLANES), 0:sw] = n_re
                bu_ref[pl.ds(r0, SUBLANES), sw:2 * sw] = n_im
                return n_re, n_im

            h_re, h_im = lax.fori_loop(
                0, tc, step,
                (hre_ref[b0:b0 + SUBLANES, s * sw:(s + 1) * sw], him_ref[b0:b0 + SUBLANES, s * sw:(s + 1) * sw]),
                unroll=4)
            hre_ref[b0:b0 + SUBLANES, s * sw:(s + 1) * sw] = h_re
            him_ref[b0:b0 + SUBLANES, s * sw:(s + 1) * sw] = h_im
        y_ref[:, s * SSM_SLAB:(s + 1) * SSM_SLAB] = jnp.dot(
            bu_ref[...].astype(BF16), wc_ref[s], preferred_element_type=F32)
    y = y_ref[...] + dsk_ref[...] * u
    gl = _gelu_tanh(y).astype(BF16)
    z = jnp.dot(gl, wglu_ref[...], preferred_element_type=F32)
    out = x + z[:, :d] * jax.nn.sigmoid(z[:, d:])
    o_ref[...] = out
    _route_rows(out, *route)
    hr_out_ref[...] = hre_ref[...]
    hi_out_ref[...] = him_ref[...]


def _ssm_kernel(x_ref, g_ref, ar_ref, ai_ref, wb_ref, wc_ref, dsk_ref, wglu_ref,
                h0r_p_ref, h0i_p_ref, h0r_s_ref, h0i_s_ref, gffn_ref, wr_ref, br_ref,
                o_ref, hr_p_ref, hi_p_ref, hr_s_ref, hi_s_ref, gate_ref, lpos_ref, cnt_ref,
                bu_ref, y_ref, hre_p_ref, him_p_ref, hre_s_ref, him_s_ref, *, n_chunks_p):
    i = pl.program_id(0)
    route = (gffn_ref, wr_ref, br_ref, gate_ref, lpos_ref, cnt_ref)
    shared = (x_ref, g_ref, ar_ref, ai_ref, wb_ref, wc_ref, dsk_ref, wglu_ref, o_ref, bu_ref, y_ref, route)

    @pl.when(i < n_chunks_p)
    def _():
        _ssm_phase(i, *shared, h0r_p_ref, h0i_p_ref, hr_p_ref, hi_p_ref, hre_p_ref, him_p_ref)

    @pl.when(i >= n_chunks_p)
    def _():
        _ssm_phase(i - n_chunks_p, *shared, h0r_s_ref, h0i_s_ref, hr_s_ref, hi_s_ref, hre_s_ref, him_s_ref)


def _ssm_call(x_all, g, ar, ai, wb, wc, dsk, wglu, h0r_p, h0i_p, h0r_s, h0i_s, route_w, *, n_p):
    n, d = x_all.shape
    rc = MIX_ROWS
    bt_p, sc = h0r_p.shape
    bt_s = h0r_s.shape[0]
    assert n_p % rc == 0 and n % rc == 0 and bt_p % SUBLANES == 0 and bt_s % SUBLANES == 0
    assert rc % bt_p == 0 and rc % bt_s == 0
    n_slab = d // SSM_SLAB
    const2 = lambda i: (0, 0)
    const3 = lambda i: (0, 0, 0)
    state = lambda bt: pl.BlockSpec((bt, sc), const2)
    state_shape = lambda bt: jax.ShapeDtypeStruct((bt, sc), F32)
    r_in, r_out, r_shape = _route_specs(n, route_w)
    return pl.pallas_call(
        functools.partial(_ssm_kernel, n_chunks_p=n_p // rc),
        grid=(n // rc,),
        in_specs=[
            pl.BlockSpec((rc, d), lambda i: (i, 0)),
            pl.BlockSpec((1, d), const2),
            pl.BlockSpec(ar.shape, const3),
            pl.BlockSpec(ai.shape, const3),
            pl.BlockSpec(wb.shape, const3, pipeline_mode=pl.Buffered(1)),
            pl.BlockSpec(wc.shape, const3, pipeline_mode=pl.Buffered(1)),
            pl.BlockSpec((1, d), const2),
            pl.BlockSpec(wglu.shape, const2, pipeline_mode=pl.Buffered(1)),
            state(bt_p), state(bt_p), state(bt_s), state(bt_s),
        ] + r_in,
        out_specs=[pl.BlockSpec((rc, d), lambda i: (i, 0)),
                   state(bt_p), state(bt_p), state(bt_s), state(bt_s)] + r_out,
        out_shape=[jax.ShapeDtypeStruct((n, d), F32),
                   state_shape(bt_p), state_shape(bt_p), state_shape(bt_s), state_shape(bt_s)] + r_shape,
        scratch_shapes=[pltpu.VMEM((rc, 2 * sc // n_slab), F32), pltpu.VMEM((rc, d), F32),
                        pltpu.VMEM((bt_p, sc), F32), pltpu.VMEM((bt_p, sc), F32),
                        pltpu.VMEM((bt_s, sc), F32), pltpu.VMEM((bt_s, sc), F32)],
        compiler_params=_params(),
        name="s5_mixer",
    )(x_all, g, ar, ai, wb, wc, dsk, wglu, h0r_p, h0i_p, h0r_s, h0i_s, *route_w)


def _ssm_weights(a_re, a_im, log_dt, b_re, b_im, c_re, c_im):
    g, p = a_re.shape
    h = b_re.shape[2]
    gs = SSM_SLAB // h
    n_slab = g // gs
    dt = jnp.exp(log_dt)[:, None]
    mag = jnp.exp(a_re * dt)
    abr = mag * jnp.cos(a_im * dt)
    abi = mag * jnp.sin(a_im * dt)
    den = a_re * a_re + a_im * a_im
    qr = ((abr - 1.0) * a_re + abi * a_im) / den
    qi = (abi * a_re - (abr - 1.0) * a_im) / den
    bbr = qr[..., None] * b_re - qi[..., None] * b_im
    bbi = qr[..., None] * b_im + qi[..., None] * b_re
    eye = jnp.eye(gs, dtype=F32)

    def in_blockdiag(m):
        m = m.reshape(n_slab, gs, p, h).transpose(0, 1, 3, 2)
        return jnp.einsum("ab,sahp->sahbp", eye, m).reshape(n_slab, gs * h, gs * p)

    def out_blockdiag(m):
        m = m.reshape(n_slab, gs, h, p).transpose(0, 1, 3, 2)
        return jnp.einsum("ab,saph->sapbh", eye, m).reshape(n_slab, gs * p, gs * h)

    wb = jnp.concatenate([in_blockdiag(bbr), in_blockdiag(bbi)], axis=-1).astype(BF16)
    wc = jnp.concatenate([out_blockdiag(c_re), out_blockdiag(-c_im)], axis=1).astype(BF16)
    ar = jnp.broadcast_to(abr.reshape(n_slab, 1, gs * p), (n_slab, SUBLANES, gs * p))
    ai = jnp.broadcast_to(abi.reshape(n_slab, 1, gs * p), (n_slab, SUBLANES, gs * p))
    return ar, ai, wb, wc


def _route_rows(rows, g_ref, wr_ref, br_ref, gate_ref, lpos_ref, cnt_ref):
    tb = TOKEN_BLOCK
    for sub in range(rows.shape[0] // tb):
        _route_block(rows[sub * tb:(sub + 1) * tb, :], g_ref[...], wr_ref[...], br_ref[...],
                     gate_ref.at[sub], lpos_ref.at[sub], cnt_ref.at[sub])


def _route_specs(n, route_w):
    g, wr_split, br = route_w
    ne, d = br.shape[0], wr_split.shape[1]
    tb = TOKEN_BLOCK
    nb, rb = n // tb, MIX_ROWS // tb
    const2 = lambda i: (0, 0)
    blk = lambda i: (i, 0, 0)
    in_specs = [pl.BlockSpec((1, d), const2), pl.BlockSpec((2 * ne, d), const2), pl.BlockSpec((ne, 1), const2)]
    out_specs = [pl.BlockSpec((rb, TOP_K, tb), blk), pl.BlockSpec((rb, TOP_K, tb), blk),
                 pl.BlockSpec((rb, ne, LANES), blk)]
    out_shape = [jax.ShapeDtypeStruct((nb, TOP_K, tb), F32), jax.ShapeDtypeStruct((nb, TOP_K, tb), jnp.int32),
                 jax.ShapeDtypeStruct((nb, ne, LANES), jnp.int32)]
    return in_specs, out_specs, out_shape


def _route_block(x, g, wr, br, gate_ref, lpos_ref, cnt_ref):
    tb = x.shape[0]
    ne = wr.shape[0] // 2
    u = _rms(x, g)
    u_hi = u.astype(BF16)
    u_lo = (u - u_hi.astype(F32)).astype(BF16)
    nt = (((1,), (1,)), ((), ()))
    by_hi = lax.dot_general(wr, u_hi, nt, preferred_element_type=F32)
    logits = by_hi[:ne] + by_hi[ne:] + lax.dot_general(wr[:ne], u_lo, nt, preferred_element_type=F32) + br
    iota_e = lax.broadcasted_iota(jnp.int32, (ne, tb), 0)
    vals, ids = [], []
    l = logits
    for _ in range(TOP_K):
        m = jnp.max(l, axis=0, keepdims=True)
        idx = jnp.min(jnp.where(l == m, iota_e, ne), axis=0, keepdims=True)
        vals.append(m)
        ids.append(idx)
        l = jnp.where(iota_e == idx, -jnp.inf, l)
    ex = [jnp.exp(v - vals[0]) for v in vals]
    den = ex[0] + ex[1] + ex[2] + ex[3]
    gate_ref[...] = jnp.concatenate([e / den for e in ex], axis=0)

    member = jnp.zeros((ne, tb), F32)
    for idx in ids:
        member = jnp.where(iota_e == idx, 1.0, member)
    tri = (lax.broadcasted_iota(jnp.int32, (tb, tb), 0) <= lax.broadcasted_iota(jnp.int32, (tb, tb), 1))
    csum = jnp.dot(member.astype(BF16), tri.astype(BF16), preferred_element_type=F32)
    cnt = csum[:, tb - 1:tb]
    pcnt = jnp.floor((cnt + (SUBLANES - 1)) / SUBLANES) * SUBLANES
    ltri = (lax.broadcasted_iota(jnp.int32, (ne, ne), 1) < lax.broadcasted_iota(jnp.int32, (ne, ne), 0))
    loff = jnp.dot(ltri.astype(BF16), jnp.broadcast_to(pcnt, (ne, LANES)).astype(BF16),
                   preferred_element_type=F32)[:, 0:1]
    lpos = []
    for idx in ids:
        sel = iota_e == idx
        lpos.append(jnp.sum(jnp.where(sel, csum - 1.0 + loff, 0.0), axis=0, keepdims=True))
    lpos_ref[...] = jnp.concatenate(lpos, axis=0).astype(jnp.int32)
    cnt_ref[...] = jnp.broadcast_to(cnt, (ne, LANES)).astype(jnp.int32)


def _local_rows(ne):
    worst = TOKEN_BLOCK * TOP_K + ne * (SUBLANES - 1)
    return -(-worst // LANES) * LANES


def _run_copies(n8, src_ref, src0, dst_ref, dst0, sem):
    top = (TOKEN_BLOCK // SUBLANES).bit_length() - 1
    for bit in range(top, -1, -1):
        size = SUBLANES << bit
        off = (n8 >> (bit + 1)) << (bit + 1 + 3)

        @pl.when(((n8 >> bit) & 1) == 1)
        def _(size=size, off=off):
            pltpu.make_async_copy(
                src_ref.at[pl.ds(pl.multiple_of(src0 + off, SUBLANES), size)],
                dst_ref.at[pl.ds(pl.multiple_of(dst0 + off, SUBLANES), size)], sem).start()


def _wait_rows(n_rows, src_ref, dst_ref, sem):
    n8 = n_rows // SUBLANES
    top = (min(src_ref.shape[0], dst_ref.shape[0]) // SUBLANES).bit_length() - 1
    for bit in range(top, -1, -1):
        size = SUBLANES << bit

        @pl.when(((n8 >> bit) & 1) == 1)
        def _(size=size):
            pltpu.make_async_copy(src_ref.at[pl.ds(0, size)], dst_ref.at[pl.ds(0, size)], sem).wait()


def _onehot_chunk(r0, tb, lp, vals):
    iota_r = lax.broadcasted_iota(jnp.int32, (PERM_CHUNK, tb), 0) + r0
    chunk = jnp.zeros((PERM_CHUNK, tb), F32)
    for k in range(TOP_K):
        chunk = jnp.where(iota_r == lp[k:k + 1, :], vals[k], chunk)
    return chunk.astype(BF16)


def _sort_kernel(pc8_ref, loff_ref, gst_ref, used_ref, gap8_ref, gapst_ref, x_ref, g_ref, lpos_ref, xs_ref,
                 buf0_ref, buf1_ref, buf2_ref, zero_ref, sem, zsem, *, ne):
    b = pl.program_id(0)
    nb = pl.num_programs(0)
    tb = x_ref.shape[0]
    bufs = (buf0_ref, buf1_ref, buf2_ref)
    nbuf = len(bufs)
    lr = buf0_ref.shape[0]

    def drain(blk, live, p):
        _wait_rows(jnp.where(live, used_ref[blk], 0), bufs[p], xs_ref, sem.at[p])

    def start_runs(blk, live, p):
        for e in range(ne):
            j = blk * ne + e
            _run_copies(jnp.where(live, pc8_ref[j], 0), bufs[p], loff_ref[j], xs_ref, gst_ref[j], sem.at[p])

    @pl.when(b == 0)
    def _():
        zero_ref[...] = jnp.zeros(zero_ref.shape, zero_ref.dtype)

        def fill(e, c):
            _run_copies(gap8_ref[e], zero_ref, 0, xs_ref, gapst_ref[e], zsem)
            return c

        lax.fori_loop(0, ne, fill, 0)

    def step(p):
        drain(jnp.maximum(b - nbuf, 0), b >= nbuf, p)
        start_runs(jnp.maximum(b - 1, 0), b >= 1, (p - 1) % nbuf)

        ub = _rms(x_ref[...], g_ref[...]).astype(BF16)
        lp = lpos_ref[0]
        for r0 in range(0, lr, PERM_CHUNK):
            bufs[p][r0:r0 + PERM_CHUNK, :] = jnp.dot(
                _onehot_chunk(r0, tb, lp, [1.0] * TOP_K), ub, preferred_element_type=F32)

        @pl.when(b == nb - 1)
        def _():
            start_runs(b, True, p)
            for back in range(nbuf - 1, -1, -1):
                drain(jnp.maximum(b - back, 0), b >= back, (p - back) % nbuf)

            def fill_done(e, c):
                _wait_rows(gap8_ref[e] * SUBLANES, zero_ref, xs_ref, zsem)
                return c

            lax.fori_loop(0, ne, fill_done, 0)

    for p in range(nbuf):
        pl.when(b % nbuf == p)(functools.partial(step, p))


def _sort_call(x, g, lpos, pc8, loff, gst, used, gap8, gapst, *, n_rows_out, ne):
    n, d = x.shape
    tb = TOKEN_BLOCK
    nb = n // tb
    lr = _local_rows(ne)
    gs = pltpu.PrefetchScalarGridSpec(
        num_scalar_prefetch=6,
        grid=(nb,),
        in_specs=[pl.BlockSpec((tb, d), lambda i, *_: (i, 0)), pl.BlockSpec((1, d), lambda i, *_: (0, 0)),
                  pl.BlockSpec((1, TOP_K, tb), lambda i, *_: (i, 0, 0))],
        out_specs=pl.BlockSpec(memory_space=pl.ANY),
        scratch_shapes=[pltpu.VMEM((lr, d), F32)] * 3 + [
            pltpu.VMEM((TOKEN_BLOCK, d), F32),
            pltpu.SemaphoreType.DMA((3,)), pltpu.SemaphoreType.DMA(())],
    )
    return pl.pallas_call(
        functools.partial(_sort_kernel, ne=ne),
        grid_spec=gs,
        out_shape=jax.ShapeDtypeStruct((n_rows_out, d), F32),
        compiler_params=_params(has_side_effects=True),
        name="moe_sort",
    )(pc8, loff, gst, used, gap8, gapst, x, g, lpos)


def _ffn_kernel(t0_ref, nt_ref, half_ref, xs_ref, wup_ref, bup_ref, wdn_ref, bdn_ref, ys_ref,
                wup_bf, wdn_bf, xbuf, ybuf, sem_in, sem_out):
    e = pl.program_id(0)
    ne = pl.num_programs(0)
    tm = xbuf.shape[1]
    f = wdn_ref.shape[0]
    t0 = t0_ref[e]
    nt = nt_ref[e]
    total = t0_ref[ne - 1] + nt_ref[ne - 1]

    def fetch(g, rows):
        return pltpu.make_async_copy(xs_ref.at[pl.ds(pl.multiple_of(g * tm, tm), rows)],
                                     xbuf.at[g % 2, pl.ds(0, rows)], sem_in.at[g % 2])

    def writeback(g, rows):
        return pltpu.make_async_copy(ybuf.at[g % 2, pl.ds(0, rows)],
                                     ys_ref.at[pl.ds(pl.multiple_of(g * tm, tm), rows)], sem_out.at[g % 2])

    def by_size(g, fn):
        @pl.when(half_ref[g] == 0)
        def _():
            fn(tm)

        @pl.when(half_ref[g] != 0)
        def _():
            fn(tm // 2)

    @pl.when(e == 0)
    def _():
        by_size(0, lambda rows: fetch(0, rows).start())

    @pl.when(nt > 0)
    def _():
        wup_bf[...] = wup_ref[...].astype(BF16)
        wdn_bf[...] = wdn_ref[...].astype(BF16)

    def tile(j, c):
        g = t0 + j

        def run(rows):
            fetch(g, rows).wait()

            @pl.when(g + 1 < total)
            def _():
                by_size(g + 1, lambda r: fetch(g + 1, r).start())

            @pl.when(g >= 2)
            def _():
                by_size(g - 2, lambda r: writeback(g - 2, r).wait())

            x = xbuf[g % 2, pl.ds(0, rows)].astype(BF16)
            gu = jnp.dot(x, wup_bf[...], preferred_element_type=F32) + bup_ref[...]
            glu = jnp.minimum(gu[:, :f], SWIGLU_LIMIT)
            lin = jnp.clip(gu[:, f:], -SWIGLU_LIMIT, SWIGLU_LIMIT)
            act = glu * jax.nn.sigmoid(SWIGLU_ALPHA * glu) * (lin + 1.0)
            ybuf[g % 2, pl.ds(0, rows)] = (
                jnp.dot(act.astype(BF16), wdn_bf[...], preferred_element_type=F32) + bdn_ref[...])
            writeback(g, rows).start()

        by_size(g, run)
        return c

    lax.fori_loop(0, nt, tile, 0)

    @pl.when(e == ne - 1)
    def _():
        @pl.when(total >= 2)
        def _():
            by_size(total - 2, lambda r: writeback(total - 2, r).wait())

        by_size(total - 1, lambda r: writeback(total - 1, r).wait())


def _ffn_call(xs, w_up, b_up, w_dn, b_dn, tile_start, tiles_e, tile_half, *, layer):
    r, d = xs.shape
    tm = FFN_TILE
    _, ne, _, f2 = w_up.shape
    f = w_dn.shape[2]
    by_expert = lambda e, *_: (layer, e, 0, 0)
    gs = pltpu.PrefetchScalarGridSpec(
        num_scalar_prefetch=3,
        grid=(ne,),
        in_specs=[
            pl.BlockSpec(memory_space=pl.ANY),
            pl.BlockSpec((None, None, d, f2), by_expert),
            pl.BlockSpec((None, None, 1, f2), by_expert),
            pl.BlockSpec((None, None, f, d), by_expert),
            pl.BlockSpec((None, None, 1, d), by_expert),
        ],
        out_specs=pl.BlockSpec(memory_space=pl.ANY),
        scratch_shapes=[pltpu.VMEM((d, f2), BF16), pltpu.VMEM((f, d), BF16),
                        pltpu.VMEM((2, tm, d), F32), pltpu.VMEM((2, tm, d), F32),
                        pltpu.SemaphoreType.DMA((2,)), pltpu.SemaphoreType.DMA((2,))],
    )
    return pl.pallas_call(
        _ffn_kernel,
        grid_spec=gs,
        out_shape=jax.ShapeDtypeStruct((r, d), F32),
        compiler_params=_params(has_side_effects=True),
        name="moe_ffn",
    )(tile_start, tiles_e, tile_half, xs, w_up, b_up[:, :, None, :], w_dn, b_dn[:, :, None, :])


def _combine_kernel(pc8_ref, loff_ref, gst_ref, used_ref, x_ref, lpos_ref, gate_ref, gfin_ref, ys_ref, o_ref,
                    buf0_ref, buf1_ref, buf2_ref, perm_ref, sem, *, ne, final_norm):
    b = pl.program_id(0)
    nb = pl.num_programs(0)
    tb = x_ref.shape[0]
    bufs = (buf0_ref, buf1_ref, buf2_ref)
    nbuf = len(bufs)
    lr = buf0_ref.shape[0]

    def fetch(blk, live, p):
        for e in range(ne):
            j = blk * ne + e
            _run_copies(jnp.where(live, pc8_ref[j], 0), ys_ref, gst_ref[j], bufs[p], loff_ref[j], sem.at[p])

    @pl.when(b == 0)
    def _():
        for buf in bufs:
            buf[...] = jnp.zeros(buf.shape, buf.dtype)
        for ahead in range(nbuf - 1):
            fetch(jnp.minimum(ahead, nb - 1), ahead < nb, ahead)

    def step(p):
        _wait_rows(used_ref[b], ys_ref, bufs[p], sem.at[p])
        fetch(jnp.minimum(b + nbuf - 1, nb - 1), b + nbuf - 1 < nb, (p + nbuf - 1) % nbuf)

        gt = gate_ref[0]
        lp = lpos_ref[0]
        for r0 in range(0, lr, PERM_CHUNK):
            perm_ref[r0:r0 + PERM_CHUNK, :] = _onehot_chunk(r0, tb, lp, [gt[k:k + 1, :] for k in range(TOP_K)])

        row = lax.broadcasted_iota(jnp.int32, (lr, 1), 0)
        ys_local = jnp.where(row < used_ref[b], bufs[p][...], 0.0).astype(BF16)
        out = lax.dot_general(perm_ref[...], ys_local, (((0,), (0,)), ((), ())), preferred_element_type=F32)
        y = x_ref[...] + out
        if final_norm:
            y = _rms(y, gfin_ref[...])
        o_ref[...] = y

    for p in range(nbuf):
        pl.when(b % nbuf == p)(functools.partial(step, p))


def _combine_call(x, lpos, gate, gfin, ys, pc8, loff, gst, used, *, ne, final_norm):
    n, d = x.shape
    tb = TOKEN_BLOCK
    nb = n // tb
    lr = _local_rows(ne)
    gs = pltpu.PrefetchScalarGridSpec(
        num_scalar_prefetch=4,
        grid=(nb,),
        in_specs=[pl.BlockSpec((tb, d), lambda i, *_: (i, 0)),
                  pl.BlockSpec((1, TOP_K, tb), lambda i, *_: (i, 0, 0)),
                  pl.BlockSpec((1, TOP_K, tb), lambda i, *_: (i, 0, 0)),
                  pl.BlockSpec((1, d), lambda i, *_: (0, 0)),
                  pl.BlockSpec(memory_space=pl.ANY)],
        out_specs=pl.BlockSpec((tb, d), lambda i, *_: (i, 0)),
        scratch_shapes=[pltpu.VMEM((lr, d), F32)] * 3 + [
            pltpu.VMEM((lr, tb), BF16), pltpu.SemaphoreType.DMA((3,))],
    )
    return pl.pallas_call(
        functools.partial(_combine_kernel, ne=ne, final_norm=final_norm),
        grid_spec=gs,
        out_shape=jax.ShapeDtypeStruct((n, d), F32),
        compiler_params=_params(),
        name="moe_combine",
    )(pc8, loff, gst, used, x, lpos, gate, gfin, ys)


def _moe_layer(x, routed, g_ffn, w_up, b_up, w_dn, b_dn, g_final, *, layer, final_norm):
    n, d = x.shape
    gate, lpos, cnt = routed
    ne = cnt.shape[1]
    tb, tm = TOKEN_BLOCK, FFN_TILE
    nb = n // tb
    cnt = cnt[:, :, 0]
    pc = (cnt + (SUBLANES - 1)) // SUBLANES * SUBLANES
    loff = jnp.cumsum(pc, axis=1) - pc
    rows_e = jnp.sum(pc, axis=0)
    tiles_e = (rows_e + tm - 1) // tm
    tile_end = jnp.cumsum(tiles_e)
    tile_start = tile_end - tiles_e
    gst = (tile_start * tm)[None, :] + jnp.cumsum(pc, axis=0) - pc
    tail = rows_e % tm
    half_e = (tail > 0) & (tail <= tm // 2)
    gap = tiles_e * tm - jnp.where(half_e, tm // 2, 0) - rows_e
    max_rows = n * TOP_K + nb * ne * (SUBLANES - 1)
    max_tiles = (max_rows + ne * (tm - SUBLANES)) // tm
    t = jnp.arange(max_tiles, dtype=jnp.int32)
    tile_half = jnp.sum((t[:, None] == (tile_end - 1)[None, :]) & half_e[None, :], axis=1)
    i32 = lambda a: a.astype(jnp.int32).reshape(-1)
    pc8, loff, gst = i32(pc // SUBLANES), i32(loff), i32(gst)
    used = i32(jnp.sum(pc, axis=1))
    xs = _sort_call(x, g_ffn, lpos, pc8, loff, gst, used, i32(gap // SUBLANES), i32(tile_start * tm + rows_e),
                    n_rows_out=max_tiles * tm, ne=ne)
    ys = _ffn_call(xs, w_up, b_up, w_dn, b_dn, i32(tile_start), i32(tiles_e), i32(tile_half), layer=layer)
    return _combine_call(x, lpos, gate, g_final, ys, pc8, loff, gst, used, ne=ne, final_norm=final_norm)


def _time_major(a):
    b, l, d = a.shape
    return a.transpose(1, 0, 2).reshape(l * b, d)


def _batch_major(a, b):
    n, d = a.shape
    return a.reshape(n // b, b, d).transpose(1, 0, 2)


def kernel(x_prompt, x_sample, cache_pool, state_ssm_re, state_ssm_im, norm_mix_g, norm_ffn_g, norm_final_g,
           pool_w, pool_scale, ssm_a_re, ssm_a_im, ssm_log_dt, ssm_b_re, ssm_b_im, ssm_c_re, ssm_c_im,
           ssm_d, ssm_glu_w, ssm_glu_gate, router_w, router_b, moe_w_up, moe_b_up, moe_w_down, moe_b_down):
    bp, lp, d = x_prompt.shape
    bs, ls, _ = x_sample.shape
    depth = norm_mix_g.shape[0]
    assert depth == 2 and cache_pool.shape[0] == 1 and state_ssm_re.shape[0] == 1
    n_p, n_s = bp * lp, bs * ls
    row = lambda v: v.reshape(1, -1)

    x = jnp.concatenate([_time_major(x_prompt), _time_major(x_sample)], axis=0)

    pw = pool_w[0].astype(BF16)
    hist_p = jnp.zeros((POOL_HIST * bp, d), F32)
    hist_s = _time_major(cache_pool[0])
    def route_w(i):
        wr_t = router_w[i].T
        wr_hi = wr_t.astype(BF16)
        wr_lo = (wr_t - wr_hi.astype(F32)).astype(BF16)
        return row(norm_ffn_g[i]), jnp.concatenate([wr_hi, wr_lo], axis=0), router_b[i][:, None]

    x1, pool_p, pool_s, *routed = _pool_call(x, hist_p, hist_s, row(norm_mix_g[0]), pw, row(pool_scale[0]),
                                             route_w(0), n_p=n_p, bt_p=bp, bt_s=bs)
    x2 = _moe_layer(x1, routed, row(norm_ffn_g[0]), moe_w_up, moe_b_up, moe_w_down, moe_b_down,
                    row(norm_final_g), layer=0, final_norm=False)

    ar, ai, wb, wc = _ssm_weights(ssm_a_re[0], ssm_a_im[0], ssm_log_dt[0], ssm_b_re[0], ssm_b_im[0],
                                  ssm_c_re[0], ssm_c_im[0])
    wglu = jnp.concatenate([ssm_glu_w[0], ssm_glu_gate[0]], axis=1).astype(BF16)
    g_groups, p_state = ssm_a_re.shape[1:]
    sc = g_groups * p_state
    zero_state = jnp.zeros((bp, sc), F32)
    x3, re_p, im_p, re_s, im_s, *routed = _ssm_call(
        x2, row(norm_mix_g[1]), ar, ai, wb, wc, row(ssm_d[0]), wglu, zero_state, zero_state,
        state_ssm_re[0].reshape(bs, sc), state_ssm_im[0].reshape(bs, sc), route_w(1), n_p=n_p)
    y = _moe_layer(x3, routed, row(norm_ffn_g[1]), moe_w_up, moe_b_up, moe_w_down, moe_b_down,
                   row(norm_final_g), layer=1, final_norm=True)

    st = lambda a, b: a.reshape(1, b, g_groups, p_state)
    return (_batch_major(y[:n_p], bp), _batch_major(y[n_p:], bs),
            _batch_major(pool_p, bp)[None], _batch_major(pool_s, bs)[None],
            st(re_p, bp), st(im_p, bp), st(re_s, bs), st(im_s, bs))
```

```python
import functools
import math

import jax
import jax.numpy as jnp
from jax import lax
from jax.experimental import pallas as pl
from jax.experimental.pallas import tpu as pltpu

F32 = jnp.float32
BF16 = jnp.bfloat16

POOL_WINDOWS = (2, 4, 8, 16)
POOL_HIST = max(POOL_WINDOWS) - 1
SSM_GROUP_CH = 16
SSM_STATE = 64
TOP_K = 4
SWIGLU_LIMIT = 7.0
SWIGLU_ALPHA = 1.702
RMS_EPS = 1e-5
PAST_LEN = 1024

SUBLANES = 8
LANES = 128
MXU_DIM = 256
VMEM_LIMIT = 56 * 1024 * 1024

MIX_ROWS = 512
TOKEN_BLOCK = 256
FFN_TILE = 512
PERM_CHUNK = 128
SSM_SLAB = MXU_DIM


def _rms(x, g):
    return x * lax.rsqrt(jnp.mean(x * x, axis=-1, keepdims=True) + RMS_EPS) * g


def _params(**kw):
    return pltpu.CompilerParams(dimension_semantics=("arbitrary",), vmem_limit_bytes=VMEM_LIMIT, **kw)


def _slab_pitch(tc):
    return tc + SUBLANES if (tc // SUBLANES) % 2 == 0 else tc


def _load_time_major(x_ref, slab_ref, xt_ref):
    bt, tc, d = x_ref.shape
    pitch = _slab_pitch(tc)
    for k in range(d // LANES):
        lanes = slice(k * LANES, (k + 1) * LANES)
        for b in range(bt):
            slab_ref[k, b * pitch:b * pitch + tc, :] = x_ref[b, :, lanes]
        for t in range(tc):
            for b0 in range(0, bt, SUBLANES):
                xt_ref[t * bt + b0:t * bt + b0 + SUBLANES, lanes] = (
                    slab_ref[k, pl.ds(b0 * pitch + t, SUBLANES, stride=pitch), :])


def _store_batch_major(y, slab_ref, o_ref):
    bt, tc, d = o_ref.shape
    for k in range(d // LANES):
        lanes = slice(k * LANES, (k + 1) * LANES)
        slab_ref[k, 0:tc * bt, :] = y[:, lanes]
        for b in range(bt):
            for t0 in range(0, tc, SUBLANES):
                o_ref[b, t0:t0 + SUBLANES, lanes] = slab_ref[k, pl.ds(t0 * bt + b, SUBLANES, stride=bt), :]


def _pool_phase(chunk, x_ref, slab_ref, xt_ref, g_ref, w_ref, scale_ref, o_ref, hist_ref, hist_out_ref, ext_ref,
                route, *, start_pos):
    bt, tc, d = x_ref.shape
    rc = bt * tc
    hr = POOL_HIST * bt
    gw = d // len(POOL_WINDOWS)

    @pl.when(chunk == 0)
    def _():
        ext_ref[0:hr, :] = hist_ref[...]

    _load_time_major(x_ref, slab_ref, xt_ref)
    x = xt_ref[...]
    u = _rms(x, g_ref[...])
    ext_ref[hr:hr + rc, :] = u
    row = lax.broadcasted_iota(jnp.int32, (rc, 1), 0)
    pos = start_pos + chunk * tc + row // bt
    outs = []
    for gi, w in enumerate(POOL_WINDOWS):
        c0 = gi * gw
        s = ext_ref[hr:hr + rc, c0:c0 + gw]
        for j in range(1, w):
            s = s + ext_ref[hr - j * bt:hr - j * bt + rc, c0:c0 + gw]
        cnt = jnp.minimum(w, pos + 1).astype(F32)
        mixed = s / cnt - u[:, c0:c0 + gw]
        outs.append(jnp.dot(mixed.astype(BF16), w_ref[gi], preferred_element_type=F32))
    y = jnp.concatenate(outs, axis=-1) * scale_ref[...]
    out = x + y
    o_ref[...] = out
    _route_rows(out, *route)
    ext_ref[0:hr, :] = ext_ref[rc:rc + hr, :]
    hist_out_ref[...] = ext_ref[0:hr, :]


def _pool_kernel(xp_ref, xs_ref, hist_p_ref, hist_s_ref, g_ref, w_ref, scale_ref, gffn_ref, wr_ref, br_ref,
                 o_ref, hout_p_ref, hout_s_ref, gate_ref, lpos_ref, cnt_ref,
                 ext_p_ref, ext_s_ref, slab_ref, xt_ref, *, n_chunks_p):
    i = pl.program_id(0)
    route = (gffn_ref, wr_ref, br_ref, gate_ref, lpos_ref, cnt_ref)

    @pl.when(i < n_chunks_p)
    def _():
        _pool_phase(i, xp_ref, slab_ref, xt_ref, g_ref, w_ref, scale_ref, o_ref, hist_p_ref, hout_p_ref,
                    ext_p_ref, route, start_pos=0)

    @pl.when(i >= n_chunks_p)
    def _():
        _pool_phase(i - n_chunks_p, xs_ref, slab_ref, xt_ref, g_ref, w_ref, scale_ref, o_ref, hist_s_ref,
                    hout_s_ref, ext_s_ref, route, start_pos=PAST_LEN)


def _pool_call(x_p, x_s, hist_p, hist_s, g, w_bf, scale, route_w):
    bt_p, l_p, d = x_p.shape
    bt_s, l_s, _ = x_s.shape
    rc = MIX_ROWS
    tc_p, tc_s = rc // bt_p, rc // bt_s
    n_p, n = bt_p * l_p, bt_p * l_p + bt_s * l_s
    ncp = n_p // rc
    hr_p, hr_s = POOL_HIST * bt_p, POOL_HIST * bt_s
    assert l_p % tc_p == 0 and l_s % tc_s == 0 and rc > max(hr_p, hr_s)
    assert tc_p % SUBLANES == 0 and tc_s % SUBLANES == 0 and bt_p % SUBLANES == 0 and bt_s % SUBLANES == 0
    slab_rows = max(bt_p * _slab_pitch(tc_p), bt_s * _slab_pitch(tc_s))
    kern = functools.partial(_pool_kernel, n_chunks_p=ncp)
    const2 = lambda i: (0, 0)
    r_in, r_out, r_shape = _route_specs(n, route_w)
    return pl.pallas_call(
        kern,
        grid=(n // rc,),
        in_specs=[
            pl.BlockSpec((bt_p, tc_p, d), lambda i: (0, jnp.minimum(i, ncp - 1), 0)),
            pl.BlockSpec((bt_s, tc_s, d), lambda i: (0, jnp.maximum(i - ncp, 0), 0)),
            pl.BlockSpec((hr_p, d), const2),
            pl.BlockSpec((hr_s, d), const2),
            pl.BlockSpec((1, d), const2),
            pl.BlockSpec(w_bf.shape, lambda i: (0, 0, 0)),
            pl.BlockSpec((1, d), const2),
        ] + r_in,
        out_specs=[pl.BlockSpec((rc, d), lambda i: (i, 0)),
                   pl.BlockSpec((hr_p, d), const2), pl.BlockSpec((hr_s, d), const2)] + r_out,
        out_shape=[jax.ShapeDtypeStruct((n, d), F32),
                   jax.ShapeDtypeStruct((hr_p, d), F32), jax.ShapeDtypeStruct((hr_s, d), F32)] + r_shape,
        scratch_shapes=[pltpu.VMEM((hr_p + rc, d), F32), pltpu.VMEM((hr_s + rc, d), F32),
                        pltpu.VMEM((d // LANES, slab_rows, LANES), F32), pltpu.VMEM((rc, d), F32)],
        compiler_params=_params(),
        name="pool_mixer",
    )(x_p, x_s, hist_p, hist_s, g, w_bf, scale, *route_w)


def _gelu_tanh(x):
    return 0.5 * x * (1.0 + jnp.tanh(math.sqrt(2.0 / math.pi) * (x + 0.044715 * (x * x * x))))


def _ssm_phase(chunk, x_ref, g_ref, ar_ref, ai_ref, wb_ref, wc_ref, dsk_ref, wglu_ref, o_ref, bu_ref, y_ref, route,
               h0r_ref, h0i_ref, hr_out_ref, hi_out_ref, hre_ref, him_ref):
    bt = hre_ref.shape[0]
    tc = x_ref.shape[0] // bt
    d = x_ref.shape[1]
    n_slab = d // SSM_SLAB
    sw = hre_ref.shape[1] // n_slab

    @pl.when(chunk == 0)
    def _():
        hre_ref[...] = h0r_ref[...]
        him_ref[...] = h0i_ref[...]

    x = x_ref[...]
    u = _rms(x, g_ref[...])
    ub = u.astype(BF16)
    for s in range(n_slab):
        bu_ref[...] = jnp.dot(ub[:, s * SSM_SLAB:(s + 1) * SSM_SLAB], wb_ref[s], preferred_element_type=F32)
        ar = ar_ref[s]
        ai = ai_ref[s]
        for hb in range(bt // SUBLANES):
            b0 = hb * SUBLANES

            def step(t, carry, b0=b0, ar=ar, ai=ai):
                h_re, h_im = carry
                r0 = pl.multiple_of(t * bt + b0, SUBLANES)
                n_re = ar * h_re - ai * h_im + bu_ref[pl.ds(r0, SUBLANES), 0:sw]
                n_im = ar * h_im + ai * h_re + bu_ref[pl.ds(r0, SUBLANES), sw:2 * sw]
                bu_ref[pl.ds(r0, SUBLANES), 0:sw] = n_re
                bu_ref[pl.ds(r0, SUBLANES), sw:2 * sw] = n_im
                return n_re, n_im

            h_re, h_im = lax.fori_loop(
                0, tc, step,
                (hre_ref[b0:b0 + SUBLANES, s * sw:(s + 1) * sw], him_ref[b0:b0 + SUBLANES, s * sw:(s + 1) * sw]),
                unroll=4)
            hre_ref[b0:b0 + SUBLANES, s * sw:(s + 1) * sw] = h_re
            him_ref[b0:b0 + SUBLANES, s * sw:(s + 1) * sw] = h_im
        y_ref[:, s * SSM_SLAB:(s + 1) * SSM_SLAB] = jnp.dot(
            bu_ref[...].astype(BF16), wc_ref[s], preferred_element_type=F32)
    y = y_ref[...] + dsk_ref[...] * u
    gl = _gelu_tanh(y).astype(BF16)
    z = jnp.dot(gl, wglu_ref[...], preferred_element_type=F32)
    out = x + z[:, :d] * jax.nn.sigmoid(z[:, d:])
    o_ref[...] = out
    _route_rows(out, *route)
    hr_out_ref[...] = hre_ref[...]
    hi_out_ref[...] = him_ref[...]


def _ssm_kernel(x_ref, g_ref, ar_ref, ai_ref, wb_ref, wc_ref, dsk_ref, wglu_ref,
                h0r_p_ref, h0i_p_ref, h0r_s_ref, h0i_s_ref, gffn_ref, wr_ref, br_ref,
                o_ref, hr_p_ref, hi_p_ref, hr_s_ref, hi_s_ref, gate_ref, lpos_ref, cnt_ref,
                bu_ref, y_ref, hre_p_ref, him_p_ref, hre_s_ref, him_s_ref, *, n_chunks_p):
    i = pl.program_id(0)
    route = (gffn_ref, wr_ref, br_ref, gate_ref, lpos_ref, cnt_ref)
    shared = (x_ref, g_ref, ar_ref, ai_ref, wb_ref, wc_ref, dsk_ref, wglu_ref, o_ref, bu_ref, y_ref, route)

    @pl.when(i < n_chunks_p)
    def _():
        _ssm_phase(i, *shared, h0r_p_ref, h0i_p_ref, hr_p_ref, hi_p_ref, hre_p_ref, him_p_ref)

    @pl.when(i >= n_chunks_p)
    def _():
        _ssm_phase(i - n_chunks_p, *shared, h0r_s_ref, h0i_s_ref, hr_s_ref, hi_s_ref, hre_s_ref, him_s_ref)


def _ssm_call(x_all, g, ar, ai, wb, wc, dsk, wglu, h0r_p, h0i_p, h0r_s, h0i_s, route_w, *, n_p):
    n, d = x_all.shape
    rc = MIX_ROWS
    bt_p, sc = h0r_p.shape
    bt_s = h0r_s.shape[0]
    assert n_p % rc == 0 and n % rc == 0 and bt_p % SUBLANES == 0 and bt_s % SUBLANES == 0
    assert rc % bt_p == 0 and rc % bt_s == 0
    n_slab = d // SSM_SLAB
    const2 = lambda i: (0, 0)
    const3 = lambda i: (0, 0, 0)
    state = lambda bt: pl.BlockSpec((bt, sc), const2)
    state_shape = lambda bt: jax.ShapeDtypeStruct((bt, sc), F32)
    r_in, r_out, r_shape = _route_specs(n, route_w)
    return pl.pallas_call(
        functools.partial(_ssm_kernel, n_chunks_p=n_p // rc),
        grid=(n // rc,),
        in_specs=[
            pl.BlockSpec((rc, d), lambda i: (i, 0)),
            pl.BlockSpec((1, d), const2),
            pl.BlockSpec(ar.shape, const3),
            pl.BlockSpec(ai.shape, const3),
            pl.BlockSpec(wb.shape, const3, pipeline_mode=pl.Buffered(1)),
            pl.BlockSpec(wc.shape, const3, pipeline_mode=pl.Buffered(1)),
            pl.BlockSpec((1, d), const2),
            pl.BlockSpec(wglu.shape, const2, pipeline_mode=pl.Buffered(1)),
            state(bt_p), state(bt_p), state(bt_s), state(bt_s),
        ] + r_in,
        out_specs=[pl.BlockSpec((rc, d), lambda i: (i, 0)),
                   state(bt_p), state(bt_p), state(bt_s), state(bt_s)] + r_out,
        out_shape=[jax.ShapeDtypeStruct((n, d), F32),
                   state_shape(bt_p), state_shape(bt_p), state_shape(bt_s), state_shape(bt_s)] + r_shape,
        scratch_shapes=[pltpu.VMEM((rc, 2 * sc // n_slab), F32), pltpu.VMEM((rc, d), F32),
                        pltpu.VMEM((bt_p, sc), F32), pltpu.VMEM((bt_p, sc), F32),
                        pltpu.VMEM((bt_s, sc), F32), pltpu.VMEM((bt_s, sc), F32)],
        compiler_params=_params(),
        name="s5_mixer",
    )(x_all, g, ar, ai, wb, wc, dsk, wglu, h0r_p, h0i_p, h0r_s, h0i_s, *route_w)


def _ssm_weights(a_re, a_im, log_dt, b_re, b_im, c_re, c_im):
    g, p = a_re.shape
    h = b_re.shape[2]
    gs = SSM_SLAB // h
    n_slab = g // gs
    dt = jnp.exp(log_dt)[:, None]
    mag = jnp.exp(a_re * dt)
    abr = mag * jnp.cos(a_im * dt)
    abi = mag * jnp.sin(a_im * dt)
    den = a_re * a_re + a_im * a_im
    qr = ((abr - 1.0) * a_re + abi * a_im) / den
    qi = (abi * a_re - (abr - 1.0) * a_im) / den
    bbr = qr[..., None] * b_re - qi[..., None] * b_im
    bbi = qr[..., None] * b_im + qi[..., None] * b_re
    eye = jnp.eye(gs, dtype=F32)

    def in_blockdiag(m):
        m = m.reshape(n_slab, gs, p, h).transpose(0, 1, 3, 2)
        return jnp.einsum("ab,sahp->sahbp", eye, m).reshape(n_slab, gs * h, gs * p)

    def out_blockdiag(m):
        m = m.reshape(n_slab, gs, h, p).transpose(0, 1, 3, 2)
        return jnp.einsum("ab,saph->sapbh", eye, m).reshape(n_slab, gs * p, gs * h)

    wb = jnp.concatenate([in_blockdiag(bbr), in_blockdiag(bbi)], axis=-1).astype(BF16)
    wc = jnp.concatenate([out_blockdiag(c_re), out_blockdiag(-c_im)], axis=1).astype(BF16)
    ar = jnp.broadcast_to(abr.reshape(n_slab, 1, gs * p), (n_slab, SUBLANES, gs * p))
    ai = jnp.broadcast_to(abi.reshape(n_slab, 1, gs * p), (n_slab, SUBLANES, gs * p))
    return ar, ai, wb, wc


def _route_rows(rows, g_ref, wr_ref, br_ref, gate_ref, lpos_ref, cnt_ref):
    tb = TOKEN_BLOCK
    for sub in range(rows.shape[0] // tb):
        _route_block(rows[sub * tb:(sub + 1) * tb, :], g_ref[...], wr_ref[...], br_ref[...],
                     gate_ref.at[sub], lpos_ref.at[sub], cnt_ref.at[sub])


def _route_specs(n, route_w):
    g, wr_split, br = route_w
    ne, d = br.shape[0], wr_split.shape[1]
    tb = TOKEN_BLOCK
    nb, rb = n // tb, MIX_ROWS // tb
    const2 = lambda i: (0, 0)
    blk = lambda i: (i, 0, 0)
    in_specs = [pl.BlockSpec((1, d), const2), pl.BlockSpec((2 * ne, d), const2), pl.BlockSpec((ne, 1), const2)]
    out_specs = [pl.BlockSpec((rb, TOP_K, tb), blk), pl.BlockSpec((rb, TOP_K, tb), blk),
                 pl.BlockSpec((rb, ne, LANES), blk)]
    out_shape = [jax.ShapeDtypeStruct((nb, TOP_K, tb), F32), jax.ShapeDtypeStruct((nb, TOP_K, tb), jnp.int32),
                 jax.ShapeDtypeStruct((nb, ne, LANES), jnp.int32)]
    return in_specs, out_specs, out_shape


def _route_block(x, g, wr, br, gate_ref, lpos_ref, cnt_ref):
    tb = x.shape[0]
    ne = wr.shape[0] // 2
    u = _rms(x, g)
    u_hi = u.astype(BF16)
    u_lo = (u - u_hi.astype(F32)).astype(BF16)
    nt = (((1,), (1,)), ((), ()))
    by_hi = lax.dot_general(wr, u_hi, nt, preferred_element_type=F32)
    logits = by_hi[:ne] + by_hi[ne:] + lax.dot_general(wr[:ne], u_lo, nt, preferred_element_type=F32) + br
    iota_e = lax.broadcasted_iota(jnp.int32, (ne, tb), 0)
    vals, ids = [], []
    l = logits
    for _ in range(TOP_K):
        m = jnp.max(l, axis=0, keepdims=True)
        idx = jnp.min(jnp.where(l == m, iota_e, ne), axis=0, keepdims=True)
        vals.append(m)
        ids.append(idx)
        l = jnp.where(iota_e == idx, -jnp.inf, l)
    ex = [jnp.exp(v - vals[0]) for v in vals]
    den = ex[0] + ex[1] + ex[2] + ex[3]
    gate_ref[...] = jnp.concatenate([e / den for e in ex], axis=0)

    member = jnp.zeros((ne, tb), F32)
    for idx in ids:
        member = jnp.where(iota_e == idx, 1.0, member)
    tri = (lax.broadcasted_iota(jnp.int32, (tb, tb), 0) <= lax.broadcasted_iota(jnp.int32, (tb, tb), 1))
    csum = jnp.dot(member.astype(BF16), tri.astype(BF16), preferred_element_type=F32)
    cnt = csum[:, tb - 1:tb]
    pcnt = jnp.floor((cnt + (SUBLANES - 1)) / SUBLANES) * SUBLANES
    ltri = (lax.broadcasted_iota(jnp.int32, (ne, ne), 1) < lax.broadcasted_iota(jnp.int32, (ne, ne), 0))
    loff = jnp.dot(ltri.astype(BF16), jnp.broadcast_to(pcnt, (ne, LANES)).astype(BF16),
                   preferred_element_type=F32)[:, 0:1]
    lpos = []
    for idx in ids:
        sel = iota_e == idx
        lpos.append(jnp.sum(jnp.where(sel, csum - 1.0 + loff, 0.0), axis=0, keepdims=True))
    lpos_ref[...] = jnp.concatenate(lpos, axis=0).astype(jnp.int32)
    cnt_ref[...] = jnp.broadcast_to(cnt, (ne, LANES)).astype(jnp.int32)


def _local_rows(ne):
    worst = TOKEN_BLOCK * TOP_K + ne * (SUBLANES - 1)
    return -(-worst // LANES) * LANES


def _run_copies(n8, src_ref, src0, dst_ref, dst0, sem):
    top = (TOKEN_BLOCK // SUBLANES).bit_length() - 1
    for bit in range(top, -1, -1):
        size = SUBLANES << bit
        off = (n8 >> (bit + 1)) << (bit + 1 + 3)

        @pl.when(((n8 >> bit) & 1) == 1)
        def _(size=size, off=off):
            pltpu.make_async_copy(
                src_ref.at[pl.ds(pl.multiple_of(src0 + off, SUBLANES), size)],
                dst_ref.at[pl.ds(pl.multiple_of(dst0 + off, SUBLANES), size)], sem).start()


def _wait_rows(n_rows, src_ref, dst_ref, sem):
    n8 = n_rows // SUBLANES
    top = (min(src_ref.shape[0], dst_ref.shape[0]) // SUBLANES).bit_length() - 1
    for bit in range(top, -1, -1):
        size = SUBLANES << bit

        @pl.when(((n8 >> bit) & 1) == 1)
        def _(size=size):
            pltpu.make_async_copy(src_ref.at[pl.ds(0, size)], dst_ref.at[pl.ds(0, size)], sem).wait()


def _onehot_chunk(r0, tb, lp, vals):
    iota_r = lax.broadcasted_iota(jnp.int32, (PERM_CHUNK, tb), 0) + r0
    chunk = jnp.zeros((PERM_CHUNK, tb), F32)
    for k in range(TOP_K):
        chunk = jnp.where(iota_r == lp[k:k + 1, :], vals[k], chunk)
    return chunk.astype(BF16)


def _sort_kernel(pc8_ref, loff_ref, gst_ref, used_ref, gap8_ref, gapst_ref, x_ref, g_ref, lpos_ref, xs_ref,
                 buf0_ref, buf1_ref, buf2_ref, zero_ref, sem, zsem, *, ne):
    b = pl.program_id(0)
    nb = pl.num_programs(0)
    tb = x_ref.shape[0]
    bufs = (buf0_ref, buf1_ref, buf2_ref)
    nbuf = len(bufs)
    lr = buf0_ref.shape[0]

    def drain(blk, live, p):
        _wait_rows(jnp.where(live, used_ref[blk], 0), bufs[p], xs_ref, sem.at[p])

    def start_runs(blk, live, p):
        for e in range(ne):
            j = blk * ne + e
            _run_copies(jnp.where(live, pc8_ref[j], 0), bufs[p], loff_ref[j], xs_ref, gst_ref[j], sem.at[p])

    @pl.when(b == 0)
    def _():
        zero_ref[...] = jnp.zeros(zero_ref.shape, zero_ref.dtype)

        def fill(e, c):
            _run_copies(gap8_ref[e], zero_ref, 0, xs_ref, gapst_ref[e], zsem)
            return c

        lax.fori_loop(0, ne, fill, 0)

    def step(p):
        drain(jnp.maximum(b - nbuf, 0), b >= nbuf, p)
        start_runs(jnp.maximum(b - 1, 0), b >= 1, (p - 1) % nbuf)

        ub = _rms(x_ref[...], g_ref[...]).astype(BF16)
        lp = lpos_ref[0]
        for r0 in range(0, lr, PERM_CHUNK):
            bufs[p][r0:r0 + PERM_CHUNK, :] = jnp.dot(
                _onehot_chunk(r0, tb, lp, [1.0] * TOP_K), ub, preferred_element_type=F32)

        @pl.when(b == nb - 1)
        def _():
            start_runs(b, True, p)
            for back in range(nbuf - 1, -1, -1):
                drain(jnp.maximum(b - back, 0), b >= back, (p - back) % nbuf)

            def fill_done(e, c):
                _wait_rows(gap8_ref[e] * SUBLANES, zero_ref, xs_ref, zsem)
                return c

            lax.fori_loop(0, ne, fill_done, 0)

    for p in range(nbuf):
        pl.when(b % nbuf == p)(functools.partial(step, p))


def _sort_call(x, g, lpos, pc8, loff, gst, used, gap8, gapst, *, n_rows_out, ne):
    n, d = x.shape
    tb = TOKEN_BLOCK
    nb = n // tb
    lr = _local_rows(ne)
    gs = pltpu.PrefetchScalarGridSpec(
        num_scalar_prefetch=6,
        grid=(nb,),
        in_specs=[pl.BlockSpec((tb, d), lambda i, *_: (i, 0)), pl.BlockSpec((1, d), lambda i, *_: (0, 0)),
                  pl.BlockSpec((1, TOP_K, tb), lambda i, *_: (i, 0, 0))],
        out_specs=pl.BlockSpec(memory_space=pl.ANY),
        scratch_shapes=[pltpu.VMEM((lr, d), F32)] * 3 + [
            pltpu.VMEM((TOKEN_BLOCK, d), F32),
            pltpu.SemaphoreType.DMA((3,)), pltpu.SemaphoreType.DMA(())],
    )
    return pl.pallas_call(
        functools.partial(_sort_kernel, ne=ne),
        grid_spec=gs,
        out_shape=jax.ShapeDtypeStruct((n_rows_out, d), F32),
        compiler_params=_params(has_side_effects=True),
        name="moe_sort",
    )(pc8, loff, gst, used, gap8, gapst, x, g, lpos)


def _ffn_kernel(t0_ref, nt_ref, half_ref, xs_ref, wup_ref, bup_ref, wdn_ref, bdn_ref, ys_ref,
                wup_bf, wdn_bf, xbuf, ybuf, sem_in, sem_out):
    e = pl.program_id(0)
    ne = pl.num_programs(0)
    tm = xbuf.shape[1]
    f = wdn_ref.shape[0]
    t0 = t0_ref[e]
    nt = nt_ref[e]
    total = t0_ref[ne - 1] + nt_ref[ne - 1]

    def fetch(g, rows):
        return pltpu.make_async_copy(xs_ref.at[pl.ds(pl.multiple_of(g * tm, tm), rows)],
                                     xbuf.at[g % 2, pl.ds(0, rows)], sem_in.at[g % 2])

    def writeback(g, rows):
        return pltpu.make_async_copy(ybuf.at[g % 2, pl.ds(0, rows)],
                                     ys_ref.at[pl.ds(pl.multiple_of(g * tm, tm), rows)], sem_out.at[g % 2])

    def by_size(g, fn):
        @pl.when(half_ref[g] == 0)
        def _():
            fn(tm)

        @pl.when(half_ref[g] != 0)
        def _():
            fn(tm // 2)

    @pl.when(e == 0)
    def _():
        by_size(0, lambda rows: fetch(0, rows).start())

    @pl.when(nt > 0)
    def _():
        wup_bf[...] = wup_ref[...].astype(BF16)
        wdn_bf[...] = wdn_ref[...].astype(BF16)

    def tile(j, c):
        g = t0 + j

        def run(rows):
            fetch(g, rows).wait()

            @pl.when(g + 1 < total)
            def _():
                by_size(g + 1, lambda r: fetch(g + 1, r).start())

            @pl.when(g >= 2)
            def _():
                by_size(g - 2, lambda r: writeback(g - 2, r).wait())

            x = xbuf[g % 2, pl.ds(0, rows)].astype(BF16)
            gu = jnp.dot(x, wup_bf[...], preferred_element_type=F32) + bup_ref[...]
            glu = jnp.minimum(gu[:, :f], SWIGLU_LIMIT)
            lin = jnp.clip(gu[:, f:], -SWIGLU_LIMIT, SWIGLU_LIMIT)
            act = glu * jax.nn.sigmoid(SWIGLU_ALPHA * glu) * (lin + 1.0)
            ybuf[g % 2, pl.ds(0, rows)] = (
                jnp.dot(act.astype(BF16), wdn_bf[...], preferred_element_type=F32) + bdn_ref[...])
            writeback(g, rows).start()

        by_size(g, run)
        return c

    lax.fori_loop(0, nt, tile, 0)

    @pl.when(e == ne - 1)
    def _():
        @pl.when(total >= 2)
        def _():
            by_size(total - 2, lambda r: writeback(total - 2, r).wait())

        by_size(total - 1, lambda r: writeback(total - 1, r).wait())


def _ffn_call(xs, w_up, b_up, w_dn, b_dn, tile_start, tiles_e, tile_half, *, layer):
    r, d = xs.shape
    tm = FFN_TILE
    _, ne, _, f2 = w_up.shape
    f = w_dn.shape[2]
    by_expert = lambda e, *_: (layer, e, 0, 0)
    gs = pltpu.PrefetchScalarGridSpec(
        num_scalar_prefetch=3,
        grid=(ne,),
        in_specs=[
            pl.BlockSpec(memory_space=pl.ANY),
            pl.BlockSpec((None, None, d, f2), by_expert),
            pl.BlockSpec((None, None, 1, f2), by_expert),
            pl.BlockSpec((None, None, f, d), by_expert),
            pl.BlockSpec((None, None, 1, d), by_expert),
        ],
        out_specs=pl.BlockSpec(memory_space=pl.ANY),
        scratch_shapes=[pltpu.VMEM((d, f2), BF16), pltpu.VMEM((f, d), BF16),
                        pltpu.VMEM((2, tm, d), F32), pltpu.VMEM((2, tm, d), F32),
                        pltpu.SemaphoreType.DMA((2,)), pltpu.SemaphoreType.DMA((2,))],
    )
    return pl.pallas_call(
        _ffn_kernel,
        grid_spec=gs,
        out_shape=jax.ShapeDtypeStruct((r, d), F32),
        compiler_params=_params(has_side_effects=True),
        name="moe_ffn",
    )(tile_start, tiles_e, tile_half, xs, w_up, b_up[:, :, None, :], w_dn, b_dn[:, :, None, :])


def _combine_kernel(pc8_ref, loff_ref, gst_ref, used_ref, x_ref, lpos_ref, gate_ref, gfin_ref, ys_ref, *rest,
                    ne, final_blocks_p):
    if final_blocks_p is None:
        o_ref, buf0_ref, buf1_ref, buf2_ref, perm_ref, sem = rest
    else:
        yp_ref, ysm_ref, buf0_ref, buf1_ref, buf2_ref, perm_ref, slab_ref, sem = rest
    b = pl.program_id(0)
    nb = pl.num_programs(0)
    tb = x_ref.shape[0]
    bufs = (buf0_ref, buf1_ref, buf2_ref)
    nbuf = len(bufs)
    lr = buf0_ref.shape[0]

    def fetch(blk, live, p):
        for e in range(ne):
            j = blk * ne + e
            _run_copies(jnp.where(live, pc8_ref[j], 0), ys_ref, gst_ref[j], bufs[p], loff_ref[j], sem.at[p])

    @pl.when(b == 0)
    def _():
        for buf in bufs:
            buf[...] = jnp.zeros(buf.shape, buf.dtype)
        for ahead in range(nbuf - 1):
            fetch(jnp.minimum(ahead, nb - 1), ahead < nb, ahead)

    def step(p):
        _wait_rows(used_ref[b], ys_ref, bufs[p], sem.at[p])
        fetch(jnp.minimum(b + nbuf - 1, nb - 1), b + nbuf - 1 < nb, (p + nbuf - 1) % nbuf)

        gt = gate_ref[0]
        lp = lpos_ref[0]
        for r0 in range(0, lr, PERM_CHUNK):
            perm_ref[r0:r0 + PERM_CHUNK, :] = _onehot_chunk(r0, tb, lp, [gt[k:k + 1, :] for k in range(TOP_K)])

        row = lax.broadcasted_iota(jnp.int32, (lr, 1), 0)
        ys_local = jnp.where(row < used_ref[b], bufs[p][...], 0.0).astype(BF16)
        out = lax.dot_general(perm_ref[...], ys_local, (((0,), (0,)), ((), ())), preferred_element_type=F32)
        y = x_ref[...] + out
        if final_blocks_p is None:
            o_ref[...] = y
        else:
            y = _rms(y, gfin_ref[...])

            @pl.when(b < final_blocks_p)
            def _():
                _store_batch_major(y, slab_ref, yp_ref)

            @pl.when(b >= final_blocks_p)
            def _():
                _store_batch_major(y, slab_ref, ysm_ref)

    for p in range(nbuf):
        pl.when(b % nbuf == p)(functools.partial(step, p))


def _combine_call(x, lpos, gate, gfin, ys, pc8, loff, gst, used, *, ne, final_shapes):
    n, d = x.shape
    tb = TOKEN_BLOCK
    nb = n // tb
    lr = _local_rows(ne)
    scratch = [pltpu.VMEM((lr, d), F32)] * 3 + [pltpu.VMEM((lr, tb), BF16)]
    if final_shapes is None:
        final_blocks_p = None
        out_specs = pl.BlockSpec((tb, d), lambda i, *_: (i, 0))
        out_shape = jax.ShapeDtypeStruct((n, d), F32)
    else:
        (bt_p, l_p), (bt_s, l_s) = final_shapes
        final_blocks_p = bt_p * l_p // tb
        nbp = final_blocks_p
        assert tb % bt_p == 0 and tb % bt_s == 0 and (tb // bt_p) % SUBLANES == 0 and (tb // bt_s) % SUBLANES == 0
        out_specs = [pl.BlockSpec((bt_p, tb // bt_p, d), lambda i, *_: (0, jnp.minimum(i, nbp - 1), 0)),
                     pl.BlockSpec((bt_s, tb // bt_s, d), lambda i, *_: (0, jnp.maximum(i - nbp, 0), 0))]
        out_shape = [jax.ShapeDtypeStruct((bt_p, l_p, d), F32), jax.ShapeDtypeStruct((bt_s, l_s, d), F32)]
        scratch.append(pltpu.VMEM((d // LANES, tb, LANES), F32))
    gs = pltpu.PrefetchScalarGridSpec(
        num_scalar_prefetch=4,
        grid=(nb,),
        in_specs=[pl.BlockSpec((tb, d), lambda i, *_: (i, 0)),
                  pl.BlockSpec((1, TOP_K, tb), lambda i, *_: (i, 0, 0)),
                  pl.BlockSpec((1, TOP_K, tb), lambda i, *_: (i, 0, 0)),
                  pl.BlockSpec((1, d), lambda i, *_: (0, 0)),
                  pl.BlockSpec(memory_space=pl.ANY)],
        out_specs=out_specs,
        scratch_shapes=scratch + [pltpu.SemaphoreType.DMA((3,))],
    )
    return pl.pallas_call(
        functools.partial(_combine_kernel, ne=ne, final_blocks_p=final_blocks_p),
        grid_spec=gs,
        out_shape=out_shape,
        compiler_params=_params(),
        name="moe_combine",
    )(pc8, loff, gst, used, x, lpos, gate, gfin, ys)


def _moe_layer(x, routed, g_ffn, w_up, b_up, w_dn, b_dn, g_final, *, layer, final_shapes):
    n, d = x.shape
    gate, lpos, cnt = routed
    ne = cnt.shape[1]
    tb, tm = TOKEN_BLOCK, FFN_TILE
    nb = n // tb
    cnt = cnt[:, :, 0]
    pc = (cnt + (SUBLANES - 1)) // SUBLANES * SUBLANES
    loff = jnp.cumsum(pc, axis=1) - pc
    rows_e = jnp.sum(pc, axis=0)
    tiles_e = (rows_e + tm - 1) // tm
    tile_end = jnp.cumsum(tiles_e)
    tile_start = tile_end - tiles_e
    gst = (tile_start * tm)[None, :] + jnp.cumsum(pc, axis=0) - pc
    tail = rows_e % tm
    half_e = (tail > 0) & (tail <= tm // 2)
    gap = tiles_e * tm - jnp.where(half_e, tm // 2, 0) - rows_e
    max_rows = n * TOP_K + nb * ne * (SUBLANES - 1)
    max_tiles = (max_rows + ne * (tm - SUBLANES)) // tm
    t = jnp.arange(max_tiles, dtype=jnp.int32)
    tile_half = jnp.sum((t[:, None] == (tile_end - 1)[None, :]) & half_e[None, :], axis=1)
    i32 = lambda a: a.astype(jnp.int32).reshape(-1)
    pc8, loff, gst = i32(pc // SUBLANES), i32(loff), i32(gst)
    used = i32(jnp.sum(pc, axis=1))
    xs = _sort_call(x, g_ffn, lpos, pc8, loff, gst, used, i32(gap // SUBLANES), i32(tile_start * tm + rows_e),
                    n_rows_out=max_tiles * tm, ne=ne)
    ys = _ffn_call(xs, w_up, b_up, w_dn, b_dn, i32(tile_start), i32(tiles_e), i32(tile_half), layer=layer)
    return _combine_call(x, lpos, gate, g_final, ys, pc8, loff, gst, used, ne=ne, final_shapes=final_shapes)


def _time_major(a):
    b, l, d = a.shape
    return a.transpose(1, 0, 2).reshape(l * b, d)


def _batch_major(a, b):
    n, d = a.shape
    return a.reshape(n // b, b, d).transpose(1, 0, 2)


def kernel(x_prompt, x_sample, cache_pool, state_ssm_re, state_ssm_im, norm_mix_g, norm_ffn_g, norm_final_g,
           pool_w, pool_scale, ssm_a_re, ssm_a_im, ssm_log_dt, ssm_b_re, ssm_b_im, ssm_c_re, ssm_c_im,
           ssm_d, ssm_glu_w, ssm_glu_gate, router_w, router_b, moe_w_up, moe_b_up, moe_w_down, moe_b_down):
    bp, lp, d = x_prompt.shape
    bs, ls, _ = x_sample.shape
    depth = norm_mix_g.shape[0]
    assert depth == 2 and cache_pool.shape[0] == 1 and state_ssm_re.shape[0] == 1
    n_p = bp * lp
    row = lambda v: v.reshape(1, -1)

    def route_w(i):
        wr_t = router_w[i].T
        wr_hi = wr_t.astype(BF16)
        wr_lo = (wr_t - wr_hi.astype(F32)).astype(BF16)
        return row(norm_ffn_g[i]), jnp.concatenate([wr_hi, wr_lo], axis=0), router_b[i][:, None]

    pw = pool_w[0].astype(BF16)
    hist_p = jnp.zeros((POOL_HIST * bp, d), F32)
    hist_s = _time_major(cache_pool[0])
    x1, pool_p, pool_s, *routed = _pool_call(x_prompt, x_sample, hist_p, hist_s, row(norm_mix_g[0]), pw,
                                             row(pool_scale[0]), route_w(0))
    x2 = _moe_layer(x1, routed, row(norm_ffn_g[0]), moe_w_up, moe_b_up, moe_w_down, moe_b_down,
                    row(norm_final_g), layer=0, final_shapes=None)

    ar, ai, wb, wc = _ssm_weights(ssm_a_re[0], ssm_a_im[0], ssm_log_dt[0], ssm_b_re[0], ssm_b_im[0],
                                  ssm_c_re[0], ssm_c_im[0])
    wglu = jnp.concatenate([ssm_glu_w[0], ssm_glu_gate[0]], axis=1).astype(BF16)
    g_groups, p_state = ssm_a_re.shape[1:]
    sc = g_groups * p_state
    zero_state = jnp.zeros((bp, sc), F32)
    x3, re_p, im_p, re_s, im_s, *routed = _ssm_call(
        x2, row(norm_mix_g[1]), ar, ai, wb, wc, row(ssm_d[0]), wglu, zero_state, zero_state,
        state_ssm_re[0].reshape(bs, sc), state_ssm_im[0].reshape(bs, sc), route_w(1), n_p=n_p)
    y_p, y_s = _moe_layer(x3, routed, row(norm_ffn_g[1]), moe_w_up, moe_b_up, moe_w_down, moe_b_down,
                          row(norm_final_g), layer=1, final_shapes=((bp, lp), (bs, ls)))

    st = lambda a, b: a.reshape(1, b, g_groups, p_state)
    return (y_p, y_s,
            _batch_major(pool_p, bp)[None], _batch_major(pool_s, bs)[None],
            st(re_p, bp), st(im_p, bp), st(re_s, bs), st(im_s, bs))
```

```python
import functools
import math

import jax
import jax.numpy as jnp
from jax import lax
from jax.experimental import pallas as pl
from jax.experimental.pallas import tpu as pltpu

F32 = jnp.float32
BF16 = jnp.bfloat16

POOL_WINDOWS = (2, 4, 8, 16)
POOL_HIST = max(POOL_WINDOWS) - 1
SSM_GROUP_CH = 16
SSM_STATE = 64
TOP_K = 4
SWIGLU_LIMIT = 7.0
SWIGLU_ALPHA = 1.702
RMS_EPS = 1e-5
PAST_LEN = 1024

SUBLANES = 8
LANES = 128
MXU_DIM = 256
VMEM_LIMIT = 56 * 1024 * 1024

MIX_ROWS = 512
TOKEN_BLOCK = 256
FFN_TILE = 512
PERM_CHUNK = 128
SSM_SLAB = MXU_DIM


def _rms(x, g):
    return x * lax.rsqrt(jnp.mean(x * x, axis=-1, keepdims=True) + RMS_EPS) * g


def _params(**kw):
    return pltpu.CompilerParams(dimension_semantics=("arbitrary",), vmem_limit_bytes=VMEM_LIMIT, **kw)


def _slab_pitch(tc):
    return tc + SUBLANES if (tc // SUBLANES) % 2 == 0 else tc


def _load_time_major(x_ref, slab_ref, xt_ref):
    bt, tc, d = x_ref.shape
    pitch = _slab_pitch(tc)
    for k in range(d // LANES):
        lanes = slice(k * LANES, (k + 1) * LANES)
        for b in range(bt):
            slab_ref[k, b * pitch:b * pitch + tc, :] = x_ref[b, :, lanes]
        for t in range(tc):
            for b0 in range(0, bt, SUBLANES):
                xt_ref[t * bt + b0:t * bt + b0 + SUBLANES, lanes] = (
                    slab_ref[k, pl.ds(b0 * pitch + t, SUBLANES, stride=pitch), :])


def _store_batch_major(y, slab_ref, o_ref):
    bt, tc, d = o_ref.shape
    for k in range(d // LANES):
        lanes = slice(k * LANES, (k + 1) * LANES)
        slab_ref[k, 0:tc * bt, :] = y[:, lanes]
        for b in range(bt):
            for t0 in range(0, tc, SUBLANES):
                o_ref[b, t0:t0 + SUBLANES, lanes] = slab_ref[k, pl.ds(t0 * bt + b, SUBLANES, stride=bt), :]


def _pool_phase(chunk, x_ref, slab_ref, xt_ref, g_ref, w_ref, scale_ref, o_ref, hist_ref, hist_out_ref, ext_ref,
                route, *, start_pos):
    bt, tc, d = x_ref.shape
    rc = bt * tc
    hr = POOL_HIST * bt
    gw = d // len(POOL_WINDOWS)

    @pl.when(chunk == 0)
    def _():
        ext_ref[0:hr, :] = hist_ref[...]

    _load_time_major(x_ref, slab_ref, xt_ref)
    x = xt_ref[...]
    u = _rms(x, g_ref[...])
    ext_ref[hr:hr + rc, :] = u
    row = lax.broadcasted_iota(jnp.int32, (rc, 1), 0)
    pos = start_pos + chunk * tc + row // bt
    outs = []
    for gi, w in enumerate(POOL_WINDOWS):
        c0 = gi * gw
        s = ext_ref[hr:hr + rc, c0:c0 + gw]
        for j in range(1, w):
            s = s + ext_ref[hr - j * bt:hr - j * bt + rc, c0:c0 + gw]
        cnt = jnp.minimum(w, pos + 1).astype(F32)
        mixed = s / cnt - u[:, c0:c0 + gw]
        outs.append(jnp.dot(mixed.astype(BF16), w_ref[gi], preferred_element_type=F32))
    y = jnp.concatenate(outs, axis=-1) * scale_ref[...]
    out = x + y
    o_ref[...] = out
    _route_rows(out, *route)
    ext_ref[0:hr, :] = ext_ref[rc:rc + hr, :]
    hist_out_ref[...] = ext_ref[0:hr, :]


def _pool_kernel(xp_ref, xs_ref, hist_p_ref, hist_s_ref, g_ref, w_ref, scale_ref, gffn_ref, wr_ref, br_ref,
                 o_ref, hout_p_ref, hout_s_ref, gate_ref, lpos_ref, cnt_ref,
                 ext_p_ref, ext_s_ref, slab_ref, xt_ref, *, n_chunks_p):
    i = pl.program_id(0)
    route = (gffn_ref, wr_ref, br_ref, gate_ref, lpos_ref, cnt_ref)

    @pl.when(i < n_chunks_p)
    def _():
        _pool_phase(i, xp_ref, slab_ref, xt_ref, g_ref, w_ref, scale_ref, o_ref, hist_p_ref, hout_p_ref,
                    ext_p_ref, route, start_pos=0)

    @pl.when(i >= n_chunks_p)
    def _():
        _pool_phase(i - n_chunks_p, xs_ref, slab_ref, xt_ref, g_ref, w_ref, scale_ref, o_ref, hist_s_ref,
                    hout_s_ref, ext_s_ref, route, start_pos=PAST_LEN)


def _pool_call(x_p, x_s, hist_p, hist_s, g, w_bf, scale, route_w):
    bt_p, l_p, d = x_p.shape
    bt_s, l_s, _ = x_s.shape
    rc = MIX_ROWS
    tc_p, tc_s = rc // bt_p, rc // bt_s
    n_p, n = bt_p * l_p, bt_p * l_p + bt_s * l_s
    ncp = n_p // rc
    hr_p, hr_s = POOL_HIST * bt_p, POOL_HIST * bt_s
    assert l_p % tc_p == 0 and l_s % tc_s == 0 and rc > max(hr_p, hr_s)
    assert tc_p % SUBLANES == 0 and tc_s % SUBLANES == 0 and bt_p % SUBLANES == 0 and bt_s % SUBLANES == 0
    slab_rows = max(bt_p * _slab_pitch(tc_p), bt_s * _slab_pitch(tc_s))
    kern = functools.partial(_pool_kernel, n_chunks_p=ncp)
    const2 = lambda i: (0, 0)
    r_in, r_out, r_shape = _route_specs(n, route_w)
    return pl.pallas_call(
        kern,
        grid=(n // rc,),
        in_specs=[
            pl.BlockSpec((bt_p, tc_p, d), lambda i: (0, jnp.minimum(i, ncp - 1), 0)),
            pl.BlockSpec((bt_s, tc_s, d), lambda i: (0, jnp.maximum(i - ncp, 0), 0)),
            pl.BlockSpec((hr_p, d), const2),
            pl.BlockSpec((hr_s, d), const2),
            pl.BlockSpec((1, d), const2),
            pl.BlockSpec(w_bf.shape, lambda i: (0, 0, 0)),
            pl.BlockSpec((1, d), const2),
        ] + r_in,
        out_specs=[pl.BlockSpec((rc, d), lambda i: (i, 0)),
                   pl.BlockSpec((hr_p, d), const2), pl.BlockSpec((hr_s, d), const2)] + r_out,
        out_shape=[jax.ShapeDtypeStruct((n, d), F32),
                   jax.ShapeDtypeStruct((hr_p, d), F32), jax.ShapeDtypeStruct((hr_s, d), F32)] + r_shape,
        scratch_shapes=[pltpu.VMEM((hr_p + rc, d), F32), pltpu.VMEM((hr_s + rc, d), F32),
                        pltpu.VMEM((d // LANES, slab_rows, LANES), F32), pltpu.VMEM((rc, d), F32)],
        compiler_params=_params(),
        name="pool_mixer",
    )(x_p, x_s, hist_p, hist_s, g, w_bf, scale, *route_w)


def _gelu_tanh(x):
    return 0.5 * x * (1.0 + jnp.tanh(math.sqrt(2.0 / math.pi) * (x + 0.044715 * (x * x * x))))


def _ssm_phase(chunk, x_ref, g_ref, ar_ref, ai_ref, wb_ref, wc_ref, dsk_ref, wglu_ref, o_ref, bu_ref, y_ref, route,
               h0r_ref, h0i_ref, hr_out_ref, hi_out_ref, hre_ref, him_ref):
    bt = hre_ref.shape[0]
    tc = x_ref.shape[0] // bt
    d = x_ref.shape[1]
    n_slab = d // SSM_SLAB
    sw = hre_ref.shape[1] // n_slab

    @pl.when(chunk == 0)
    def _():
        hre_ref[...] = h0r_ref[...]
        him_ref[...] = h0i_ref[...]

    x = x_ref[...]
    u = _rms(x, g_ref[...])
    ub = u.astype(BF16)
    for s in range(n_slab):
        bu_ref[s] = jnp.dot(ub[:, s * SSM_SLAB:(s + 1) * SSM_SLAB], wb_ref[s], preferred_element_type=F32)
    for s in range(n_slab):
        ar = ar_ref[s]
        ai = ai_ref[s]
        cols = slice(s * sw, (s + 1) * sw)
        for b0 in range(0, bt, SUBLANES):
            h_re = hre_ref[b0:b0 + SUBLANES, cols]
            h_im = him_ref[b0:b0 + SUBLANES, cols]
            for t in range(tc):
                rows = slice(t * bt + b0, t * bt + b0 + SUBLANES)
                h_re, h_im = (ar * h_re - ai * h_im + bu_ref[s, rows, 0:sw],
                              ar * h_im + ai * h_re + bu_ref[s, rows, sw:2 * sw])
                bu_ref[s, rows, 0:sw] = h_re
                bu_ref[s, rows, sw:2 * sw] = h_im
            hre_ref[b0:b0 + SUBLANES, cols] = h_re
            him_ref[b0:b0 + SUBLANES, cols] = h_im
    for s in range(n_slab):
        y_ref[:, s * SSM_SLAB:(s + 1) * SSM_SLAB] = jnp.dot(
            bu_ref[s].astype(BF16), wc_ref[s], preferred_element_type=F32)
    y = y_ref[...] + dsk_ref[...] * u
    gl = _gelu_tanh(y).astype(BF16)
    z = jnp.dot(gl, wglu_ref[...], preferred_element_type=F32)
    out = x + z[:, :d] * jax.nn.sigmoid(z[:, d:])
    o_ref[...] = out
    _route_rows(out, *route)
    hr_out_ref[...] = hre_ref[...]
    hi_out_ref[...] = him_ref[...]


def _ssm_kernel(x_ref, g_ref, ar_ref, ai_ref, wb_ref, wc_ref, dsk_ref, wglu_ref,
                h0r_p_ref, h0i_p_ref, h0r_s_ref, h0i_s_ref, gffn_ref, wr_ref, br_ref,
                o_ref, hr_p_ref, hi_p_ref, hr_s_ref, hi_s_ref, gate_ref, lpos_ref, cnt_ref,
                bu_ref, y_ref, hre_p_ref, him_p_ref, hre_s_ref, him_s_ref, *, n_chunks_p):
    i = pl.program_id(0)
    route = (gffn_ref, wr_ref, br_ref, gate_ref, lpos_ref, cnt_ref)
    shared = (x_ref, g_ref, ar_ref, ai_ref, wb_ref, wc_ref, dsk_ref, wglu_ref, o_ref, bu_ref, y_ref, route)

    @pl.when(i < n_chunks_p)
    def _():
        _ssm_phase(i, *shared, h0r_p_ref, h0i_p_ref, hr_p_ref, hi_p_ref, hre_p_ref, him_p_ref)

    @pl.when(i >= n_chunks_p)
    def _():
        _ssm_phase(i - n_chunks_p, *shared, h0r_s_ref, h0i_s_ref, hr_s_ref, hi_s_ref, hre_s_ref, him_s_ref)


def _ssm_call(x_all, g, ar, ai, wb, wc, dsk, wglu, h0r_p, h0i_p, h0r_s, h0i_s, route_w, *, n_p):
    n, d = x_all.shape
    rc = MIX_ROWS
    bt_p, sc = h0r_p.shape
    bt_s = h0r_s.shape[0]
    assert n_p % rc == 0 and n % rc == 0 and bt_p % SUBLANES == 0 and bt_s % SUBLANES == 0
    assert rc % bt_p == 0 and rc % bt_s == 0
    n_slab = d // SSM_SLAB
    const2 = lambda i: (0, 0)
    const3 = lambda i: (0, 0, 0)
    state = lambda bt: pl.BlockSpec((bt, sc), const2)
    state_shape = lambda bt: jax.ShapeDtypeStruct((bt, sc), F32)
    r_in, r_out, r_shape = _route_specs(n, route_w)
    return pl.pallas_call(
        functools.partial(_ssm_kernel, n_chunks_p=n_p // rc),
        grid=(n // rc,),
        in_specs=[
            pl.BlockSpec((rc, d), lambda i: (i, 0)),
            pl.BlockSpec((1, d), const2),
            pl.BlockSpec(ar.shape, const3),
            pl.BlockSpec(ai.shape, const3),
            pl.BlockSpec(wb.shape, const3, pipeline_mode=pl.Buffered(1)),
            pl.BlockSpec(wc.shape, const3, pipeline_mode=pl.Buffered(1)),
            pl.BlockSpec((1, d), const2),
            pl.BlockSpec(wglu.shape, const2, pipeline_mode=pl.Buffered(1)),
            state(bt_p), state(bt_p), state(bt_s), state(bt_s),
        ] + r_in,
        out_specs=[pl.BlockSpec((rc, d), lambda i: (i, 0)),
                   state(bt_p), state(bt_p), state(bt_s), state(bt_s)] + r_out,
        out_shape=[jax.ShapeDtypeStruct((n, d), F32),
                   state_shape(bt_p), state_shape(bt_p), state_shape(bt_s), state_shape(bt_s)] + r_shape,
        scratch_shapes=[pltpu.VMEM((n_slab, rc, 2 * sc // n_slab), F32), pltpu.VMEM((rc, d), F32),
                        pltpu.VMEM((bt_p, sc), F32), pltpu.VMEM((bt_p, sc), F32),
                        pltpu.VMEM((bt_s, sc), F32), pltpu.VMEM((bt_s, sc), F32)],
        compiler_params=_params(),
        name="s5_mixer",
    )(x_all, g, ar, ai, wb, wc, dsk, wglu, h0r_p, h0i_p, h0r_s, h0i_s, *route_w)


def _ssm_weights(a_re, a_im, log_dt, b_re, b_im, c_re, c_im):
    g, p = a_re.shape
    h = b_re.shape[2]
    gs = SSM_SLAB // h
    n_slab = g // gs
    dt = jnp.exp(log_dt)[:, None]
    mag = jnp.exp(a_re * dt)
    abr = mag * jnp.cos(a_im * dt)
    abi = mag * jnp.sin(a_im * dt)
    den = a_re * a_re + a_im * a_im
    qr = ((abr - 1.0) * a_re + abi * a_im) / den
    qi = (abi * a_re - (abr - 1.0) * a_im) / den
    bbr = qr[..., None] * b_re - qi[..., None] * b_im
    bbi = qr[..., None] * b_im + qi[..., None] * b_re
    eye = jnp.eye(gs, dtype=F32)

    def in_blockdiag(m):
        m = m.reshape(n_slab, gs, p, h).transpose(0, 1, 3, 2)
        return jnp.einsum("ab,sahp->sahbp", eye, m).reshape(n_slab, gs * h, gs * p)

    def out_blockdiag(m):
        m = m.reshape(n_slab, gs, h, p).transpose(0, 1, 3, 2)
        return jnp.einsum("ab,saph->sapbh", eye, m).reshape(n_slab, gs * p, gs * h)

    wb = jnp.concatenate([in_blockdiag(bbr), in_blockdiag(bbi)], axis=-1).astype(BF16)
    wc = jnp.concatenate([out_blockdiag(c_re), out_blockdiag(-c_im)], axis=1).astype(BF16)
    ar = jnp.broadcast_to(abr.reshape(n_slab, 1, gs * p), (n_slab, SUBLANES, gs * p))
    ai = jnp.broadcast_to(abi.reshape(n_slab, 1, gs * p), (n_slab, SUBLANES, gs * p))
    return ar, ai, wb, wc


def _route_rows(rows, g_ref, wr_ref, br_ref, gate_ref, lpos_ref, cnt_ref):
    tb = TOKEN_BLOCK
    for sub in range(rows.shape[0] // tb):
        _route_block(rows[sub * tb:(sub + 1) * tb, :], g_ref[...], wr_ref[...], br_ref[...],
                     gate_ref.at[sub], lpos_ref.at[sub], cnt_ref.at[sub])


def _route_specs(n, route_w):
    g, wr_split, br = route_w
    ne, d = br.shape[0], wr_split.shape[1]
    tb = TOKEN_BLOCK
    nb, rb = n // tb, MIX_ROWS // tb
    const2 = lambda i: (0, 0)
    blk = lambda i: (i, 0, 0)
    in_specs = [pl.BlockSpec((1, d), const2), pl.BlockSpec((2 * ne, d), const2), pl.BlockSpec((ne, 1), const2)]
    out_specs = [pl.BlockSpec((rb, TOP_K, tb), blk), pl.BlockSpec((rb, TOP_K, tb), blk),
                 pl.BlockSpec((rb, ne, LANES), blk)]
    out_shape = [jax.ShapeDtypeStruct((nb, TOP_K, tb), F32), jax.ShapeDtypeStruct((nb, TOP_K, tb), jnp.int32),
                 jax.ShapeDtypeStruct((nb, ne, LANES), jnp.int32)]
    return in_specs, out_specs, out_shape


def _route_block(x, g, wr, br, gate_ref, lpos_ref, cnt_ref):
    tb = x.shape[0]
    ne = wr.shape[0] // 2
    u = _rms(x, g)
    u_hi = u.astype(BF16)
    u_lo = (u - u_hi.astype(F32)).astype(BF16)
    nt = (((1,), (1,)), ((), ()))
    by_hi = lax.dot_general(wr, u_hi, nt, preferred_element_type=F32)
    logits = by_hi[:ne] + by_hi[ne:] + lax.dot_general(wr[:ne], u_lo, nt, preferred_element_type=F32) + br
    iota_e = lax.broadcasted_iota(jnp.int32, (ne, tb), 0)
    vals, ids = [], []
    l = logits
    for _ in range(TOP_K):
        m = jnp.max(l, axis=0, keepdims=True)
        idx = jnp.min(jnp.where(l == m, iota_e, ne), axis=0, keepdims=True)
        vals.append(m)
        ids.append(idx)
        l = jnp.where(iota_e == idx, -jnp.inf, l)
    ex = [jnp.exp(v - vals[0]) for v in vals]
    den = ex[0] + ex[1] + ex[2] + ex[3]
    gate_ref[...] = jnp.concatenate([e / den for e in ex], axis=0)

    member = jnp.zeros((ne, tb), F32)
    for idx in ids:
        member = jnp.where(iota_e == idx, 1.0, member)
    tri = (lax.broadcasted_iota(jnp.int32, (tb, tb), 0) <= lax.broadcasted_iota(jnp.int32, (tb, tb), 1))
    csum = jnp.dot(member.astype(BF16), tri.astype(BF16), preferred_element_type=F32)
    cnt = csum[:, tb - 1:tb]
    pcnt = jnp.floor((cnt + (SUBLANES - 1)) / SUBLANES) * SUBLANES
    ltri = (lax.broadcasted_iota(jnp.int32, (ne, ne), 1) < lax.broadcasted_iota(jnp.int32, (ne, ne), 0))
    loff = jnp.dot(ltri.astype(BF16), jnp.broadcast_to(pcnt, (ne, LANES)).astype(BF16),
                   preferred_element_type=F32)[:, 0:1]
    lpos = []
    for idx in ids:
        sel = iota_e == idx
        lpos.append(jnp.sum(jnp.where(sel, csum - 1.0 + loff, 0.0), axis=0, keepdims=True))
    lpos_ref[...] = jnp.concatenate(lpos, axis=0).astype(jnp.int32)
    cnt_ref[...] = jnp.broadcast_to(cnt, (ne, LANES)).astype(jnp.int32)


def _local_rows(ne):
    worst = TOKEN_BLOCK * TOP_K + ne * (SUBLANES - 1)
    return -(-worst // LANES) * LANES


def _run_copies(n8, src_ref, src0, dst_ref, dst0, sem):
    top = (TOKEN_BLOCK // SUBLANES).bit_length() - 1
    for bit in range(top, -1, -1):
        size = SUBLANES << bit
        off = (n8 >> (bit + 1)) << (bit + 1 + 3)

        @pl.when(((n8 >> bit) & 1) == 1)
        def _(size=size, off=off):
            pltpu.make_async_copy(
                src_ref.at[pl.ds(pl.multiple_of(src0 + off, SUBLANES), size)],
                dst_ref.at[pl.ds(pl.multiple_of(dst0 + off, SUBLANES), size)], sem).start()


def _wait_rows(n_rows, src_ref, dst_ref, sem):
    n8 = n_rows // SUBLANES
    top = (min(src_ref.shape[0], dst_ref.shape[0]) // SUBLANES).bit_length() - 1
    for bit in range(top, -1, -1):
        size = SUBLANES << bit

        @pl.when(((n8 >> bit) & 1) == 1)
        def _(size=size):
            pltpu.make_async_copy(src_ref.at[pl.ds(0, size)], dst_ref.at[pl.ds(0, size)], sem).wait()


def _onehot_chunk(r0, tb, lp, vals):
    iota_r = lax.broadcasted_iota(jnp.int32, (PERM_CHUNK, tb), 0) + r0
    chunk = jnp.zeros((PERM_CHUNK, tb), F32)
    for k in range(TOP_K):
        chunk = jnp.where(iota_r == lp[k:k + 1, :], vals[k], chunk)
    return chunk.astype(BF16)


def _sort_kernel(pc8_ref, loff_ref, gst_ref, used_ref, gap8_ref, gapst_ref, x_ref, g_ref, lpos_ref, xs_ref,
                 buf0_ref, buf1_ref, buf2_ref, zero_ref, sem, zsem, *, ne):
    b = pl.program_id(0)
    nb = pl.num_programs(0)
    tb = x_ref.shape[0]
    bufs = (buf0_ref, buf1_ref, buf2_ref)
    nbuf = len(bufs)
    lr = buf0_ref.shape[0]

    def drain(blk, live, p):
        _wait_rows(jnp.where(live, used_ref[blk], 0), bufs[p], xs_ref, sem.at[p])

    def start_runs(blk, live, p):
        for e in range(ne):
            j = blk * ne + e
            _run_copies(jnp.where(live, pc8_ref[j], 0), bufs[p], loff_ref[j], xs_ref, gst_ref[j], sem.at[p])

    @pl.when(b == 0)
    def _():
        zero_ref[...] = jnp.zeros(zero_ref.shape, zero_ref.dtype)

        def fill(e, c):
            _run_copies(gap8_ref[e], zero_ref, 0, xs_ref, gapst_ref[e], zsem)
            return c

        lax.fori_loop(0, ne, fill, 0)

    def step(p):
        drain(jnp.maximum(b - nbuf, 0), b >= nbuf, p)
        start_runs(jnp.maximum(b - 1, 0), b >= 1, (p - 1) % nbuf)

        ub = _rms(x_ref[...], g_ref[...]).astype(BF16)
        lp = lpos_ref[0]
        for r0 in range(0, lr, PERM_CHUNK):
            bufs[p][r0:r0 + PERM_CHUNK, :] = jnp.dot(
                _onehot_chunk(r0, tb, lp, [1.0] * TOP_K), ub, preferred_element_type=F32)

        @pl.when(b == nb - 1)
        def _():
            start_runs(b, True, p)
            for back in range(nbuf - 1, -1, -1):
                drain(jnp.maximum(b - back, 0), b >= back, (p - back) % nbuf)

            def fill_done(e, c):
                _wait_rows(gap8_ref[e] * SUBLANES, zero_ref, xs_ref, zsem)
                return c

            lax.fori_loop(0, ne, fill_done, 0)

    for p in range(nbuf):
        pl.when(b % nbuf == p)(functools.partial(step, p))


def _sort_call(x, g, lpos, pc8, loff, gst, used, gap8, gapst, *, n_rows_out, ne):
    n, d = x.shape
    tb = TOKEN_BLOCK
    nb = n // tb
    lr = _local_rows(ne)
    gs = pltpu.PrefetchScalarGridSpec(
        num_scalar_prefetch=6,
        grid=(nb,),
        in_specs=[pl.BlockSpec((tb, d), lambda i, *_: (i, 0)), pl.BlockSpec((1, d), lambda i, *_: (0, 0)),
                  pl.BlockSpec((1, TOP_K, tb), lambda i, *_: (i, 0, 0))],
        out_specs=pl.BlockSpec(memory_space=pl.ANY),
        scratch_shapes=[pltpu.VMEM((lr, d), F32)] * 3 + [
            pltpu.VMEM((TOKEN_BLOCK, d), F32),
            pltpu.SemaphoreType.DMA((3,)), pltpu.SemaphoreType.DMA(())],
    )
    return pl.pallas_call(
        functools.partial(_sort_kernel, ne=ne),
        grid_spec=gs,
        out_shape=jax.ShapeDtypeStruct((n_rows_out, d), F32),
        compiler_params=_params(has_side_effects=True),
        name="moe_sort",
    )(pc8, loff, gst, used, gap8, gapst, x, g, lpos)


def _ffn_kernel(t0_ref, nt_ref, half_ref, xs_ref, wup_ref, bup_ref, wdn_ref, bdn_ref, ys_ref,
                wup_bf, wdn_bf, xbuf, ybuf, sem_in, sem_out):
    e = pl.program_id(0)
    ne = pl.num_programs(0)
    tm = xbuf.shape[1]
    f = wdn_ref.shape[0]
    t0 = t0_ref[e]
    nt = nt_ref[e]
    total = t0_ref[ne - 1] + nt_ref[ne - 1]

    def fetch(g, rows):
        return pltpu.make_async_copy(xs_ref.at[pl.ds(pl.multiple_of(g * tm, tm), rows)],
                                     xbuf.at[g % 2, pl.ds(0, rows)], sem_in.at[g % 2])

    def writeback(g, rows):
        return pltpu.make_async_copy(ybuf.at[g % 2, pl.ds(0, rows)],
                                     ys_ref.at[pl.ds(pl.multiple_of(g * tm, tm), rows)], sem_out.at[g % 2])

    def by_size(g, fn):
        @pl.when(half_ref[g] == 0)
        def _():
            fn(tm)

        @pl.when(half_ref[g] != 0)
        def _():
            fn(tm // 2)

    @pl.when(e == 0)
    def _():
        by_size(0, lambda rows: fetch(0, rows).start())

    @pl.when(nt > 0)
    def _():
        wup_bf[...] = wup_ref[...].astype(BF16)
        wdn_bf[...] = wdn_ref[...].astype(BF16)

    def tile(j, c):
        g = t0 + j

        def run(rows):
            fetch(g, rows).wait()

            @pl.when(g + 1 < total)
            def _():
                by_size(g + 1, lambda r: fetch(g + 1, r).start())

            @pl.when(g >= 2)
            def _():
                by_size(g - 2, lambda r: writeback(g - 2, r).wait())

            x = xbuf[g % 2, pl.ds(0, rows)].astype(BF16)
            gu = jnp.dot(x, wup_bf[...], preferred_element_type=F32) + bup_ref[...]
            glu = jnp.minimum(gu[:, :f], SWIGLU_LIMIT)
            lin = jnp.clip(gu[:, f:], -SWIGLU_LIMIT, SWIGLU_LIMIT)
            act = glu * jax.nn.sigmoid(SWIGLU_ALPHA * glu) * (lin + 1.0)
            ybuf[g % 2, pl.ds(0, rows)] = (
                jnp.dot(act.astype(BF16), wdn_bf[...], preferred_element_type=F32) + bdn_ref[...])
            writeback(g, rows).start()

        by_size(g, run)
        return c

    lax.fori_loop(0, nt, tile, 0)

    @pl.when(e == ne - 1)
    def _():
        @pl.when(total >= 2)
        def _():
            by_size(total - 2, lambda r: writeback(total - 2, r).wait())

        by_size(total - 1, lambda r: writeback(total - 1, r).wait())


def _ffn_call(xs, w_up, b_up, w_dn, b_dn, tile_start, tiles_e, tile_half, *, layer):
    r, d = xs.shape
    tm = FFN_TILE
    _, ne, _, f2 = w_up.shape
    f = w_dn.shape[2]
    by_expert = lambda e, *_: (layer, e, 0, 0)
    gs = pltpu.PrefetchScalarGridSpec(
        num_scalar_prefetch=3,
        grid=(ne,),
        in_specs=[
            pl.BlockSpec(memory_space=pl.ANY),
            pl.BlockSpec((None, None, d, f2), by_expert),
            pl.BlockSpec((None, None, 1, f2), by_expert),
            pl.BlockSpec((None, None, f, d), by_expert),
            pl.BlockSpec((None, None, 1, d), by_expert),
        ],
        out_specs=pl.BlockSpec(memory_space=pl.ANY),
        scratch_shapes=[pltpu.VMEM((d, f2), BF16), pltpu.VMEM((f, d), BF16),
                        pltpu.VMEM((2, tm, d), F32), pltpu.VMEM((2, tm, d), F32),
                        pltpu.SemaphoreType.DMA((2,)), pltpu.SemaphoreType.DMA((2,))],
    )
    return pl.pallas_call(
        _ffn_kernel,
        grid_spec=gs,
        out_shape=jax.ShapeDtypeStruct((r, d), F32),
        compiler_params=_params(has_side_effects=True),
        name="moe_ffn",
    )(tile_start, tiles_e, tile_half, xs, w_up, b_up[:, :, None, :], w_dn, b_dn[:, :, None, :])


def _combine_kernel(pc8_ref, loff_ref, gst_ref, used_ref, x_ref, lpos_ref, gate_ref, gfin_ref, ys_ref, *rest,
                    ne, final_blocks_p):
    if final_blocks_p is None:
        o_ref, buf0_ref, buf1_ref, buf2_ref, perm_ref, sem = rest
    else:
        yp_ref, ysm_ref, buf0_ref, buf1_ref, buf2_ref, perm_ref, slab_ref, sem = rest
    b = pl.program_id(0)
    nb = pl.num_programs(0)
    tb = x_ref.shape[0]
    bufs = (buf0_ref, buf1_ref, buf2_ref)
    nbuf = len(bufs)
    lr = buf0_ref.shape[0]

    def fetch(blk, live, p):
        for e in range(ne):
            j = blk * ne + e
            _run_copies(jnp.where(live, pc8_ref[j], 0), ys_ref, gst_ref[j], bufs[p], loff_ref[j], sem.at[p])

    @pl.when(b == 0)
    def _():
        for buf in bufs:
            buf[...] = jnp.zeros(buf.shape, buf.dtype)
        for ahead in range(nbuf - 1):
            fetch(jnp.minimum(ahead, nb - 1), ahead < nb, ahead)

    def step(p):
        _wait_rows(used_ref[b], ys_ref, bufs[p], sem.at[p])
        fetch(jnp.minimum(b + nbuf - 1, nb - 1), b + nbuf - 1 < nb, (p + nbuf - 1) % nbuf)

        gt = gate_ref[0]
        lp = lpos_ref[0]
        for r0 in range(0, lr, PERM_CHUNK):
            perm_ref[r0:r0 + PERM_CHUNK, :] = _onehot_chunk(r0, tb, lp, [gt[k:k + 1, :] for k in range(TOP_K)])

        row = lax.broadcasted_iota(jnp.int32, (lr, 1), 0)
        ys_local = jnp.where(row < used_ref[b], bufs[p][...], 0.0).astype(BF16)
        out = lax.dot_general(perm_ref[...], ys_local, (((0,), (0,)), ((), ())), preferred_element_type=F32)
        y = x_ref[...] + out
        if final_blocks_p is None:
            o_ref[...] = y
        else:
            y = _rms(y, gfin_ref[...])

            @pl.when(b < final_blocks_p)
            def _():
                _store_batch_major(y, slab_ref, yp_ref)

            @pl.when(b >= final_blocks_p)
            def _():
                _store_batch_major(y, slab_ref, ysm_ref)

    for p in range(nbuf):
        pl.when(b % nbuf == p)(functools.partial(step, p))


def _combine_call(x, lpos, gate, gfin, ys, pc8, loff, gst, used, *, ne, final_shapes):
    n, d = x.shape
    tb = TOKEN_BLOCK
    nb = n // tb
    lr = _local_rows(ne)
    scratch = [pltpu.VMEM((lr, d), F32)] * 3 + [pltpu.VMEM((lr, tb), BF16)]
    if final_shapes is None:
        final_blocks_p = None
        out_specs = pl.BlockSpec((tb, d), lambda i, *_: (i, 0))
        out_shape = jax.ShapeDtypeStruct((n, d), F32)
    else:
        (bt_p, l_p), (bt_s, l_s) = final_shapes
        final_blocks_p = bt_p * l_p // tb
        nbp = final_blocks_p
        assert tb % bt_p == 0 and tb % bt_s == 0 and (tb // bt_p) % SUBLANES == 0 and (tb // bt_s) % SUBLANES == 0
        out_specs = [pl.BlockSpec((bt_p, tb // bt_p, d), lambda i, *_: (0, jnp.minimum(i, nbp - 1), 0)),
                     pl.BlockSpec((bt_s, tb // bt_s, d), lambda i, *_: (0, jnp.maximum(i - nbp, 0), 0))]
        out_shape = [jax.ShapeDtypeStruct((bt_p, l_p, d), F32), jax.ShapeDtypeStruct((bt_s, l_s, d), F32)]
        scratch.append(pltpu.VMEM((d // LANES, tb, LANES), F32))
    gs = pltpu.PrefetchScalarGridSpec(
        num_scalar_prefetch=4,
        grid=(nb,),
        in_specs=[pl.BlockSpec((tb, d), lambda i, *_: (i, 0)),
                  pl.BlockSpec((1, TOP_K, tb), lambda i, *_: (i, 0, 0)),
                  pl.BlockSpec((1, TOP_K, tb), lambda i, *_: (i, 0, 0)),
                  pl.BlockSpec((1, d), lambda i, *_: (0, 0)),
                  pl.BlockSpec(memory_space=pl.ANY)],
        out_specs=out_specs,
        scratch_shapes=scratch + [pltpu.SemaphoreType.DMA((3,))],
    )
    return pl.pallas_call(
        functools.partial(_combine_kernel, ne=ne, final_blocks_p=final_blocks_p),
        grid_spec=gs,
        out_shape=out_shape,
        compiler_params=_params(),
        name="moe_combine",
    )(pc8, loff, gst, used, x, lpos, gate, gfin, ys)


def _moe_layer(x, routed, g_ffn, w_up, b_up, w_dn, b_dn, g_final, *, layer, final_shapes):
    n, d = x.shape
    gate, lpos, cnt = routed
    ne = cnt.shape[1]
    tb, tm = TOKEN_BLOCK, FFN_TILE
    nb = n // tb
    cnt = cnt[:, :, 0]
    pc = (cnt + (SUBLANES - 1)) // SUBLANES * SUBLANES
    loff = jnp.cumsum(pc, axis=1) - pc
    rows_e = jnp.sum(pc, axis=0)
    tiles_e = (rows_e + tm - 1) // tm
    tile_end = jnp.cumsum(tiles_e)
    tile_start = tile_end - tiles_e
    gst = (tile_start * tm)[None, :] + jnp.cumsum(pc, axis=0) - pc
    tail = rows_e % tm
    half_e = (tail > 0) & (tail <= tm // 2)
    gap = tiles_e * tm - jnp.where(half_e, tm // 2, 0) - rows_e
    max_rows = n * TOP_K + nb * ne * (SUBLANES - 1)
    max_tiles = (max_rows + ne * (tm - SUBLANES)) // tm
    t = jnp.arange(max_tiles, dtype=jnp.int32)
    tile_half = jnp.sum((t[:, None] == (tile_end - 1)[None, :]) & half_e[None, :], axis=1)
    i32 = lambda a: a.astype(jnp.int32).reshape(-1)
    pc8, loff, gst = i32(pc // SUBLANES), i32(loff), i32(gst)
    used = i32(jnp.sum(pc, axis=1))
    xs = _sort_call(x, g_ffn, lpos, pc8, loff, gst, used, i32(gap // SUBLANES), i32(tile_start * tm + rows_e),
                    n_rows_out=max_tiles * tm, ne=ne)
    ys = _ffn_call(xs, w_up, b_up, w_dn, b_dn, i32(tile_start), i32(tiles_e), i32(tile_half), layer=layer)
    return _combine_call(x, lpos, gate, g_final, ys, pc8, loff, gst, used, ne=ne, final_shapes=final_shapes)


def _time_major(a):
    b, l, d = a.shape
    return a.transpose(1, 0, 2).reshape(l * b, d)


def _batch_major(a, b):
    n, d = a.shape
    return a.reshape(n // b, b, d).transpose(1, 0, 2)


def kernel(x_prompt, x_sample, cache_pool, state_ssm_re, state_ssm_im, norm_mix_g, norm_ffn_g, norm_final_g,
           pool_w, pool_scale, ssm_a_re, ssm_a_im, ssm_log_dt, ssm_b_re, ssm_b_im, ssm_c_re, ssm_c_im,
           ssm_d, ssm_glu_w, ssm_glu_gate, router_w, router_b, moe_w_up, moe_b_up, moe_w_down, moe_b_down):
    bp, lp, d = x_prompt.shape
    bs, ls, _ = x_sample.shape
    depth = norm_mix_g.shape[0]
    assert depth == 2 and cache_pool.shape[0] == 1 and state_ssm_re.shape[0] == 1
    n_p = bp * lp
    row = lambda v: v.reshape(1, -1)

    def route_w(i):
        wr_t = router_w[i].T
        wr_hi = wr_t.astype(BF16)
        wr_lo = (wr_t - wr_hi.astype(F32)).astype(BF16)
        return row(norm_ffn_g[i]), jnp.concatenate([wr_hi, wr_lo], axis=0), router_b[i][:, None]

    pw = pool_w[0].astype(BF16)
    hist_p = jnp.zeros((POOL_HIST * bp, d), F32)
    hist_s = _time_major(cache_pool[0])
    x1, pool_p, pool_s, *routed = _pool_call(x_prompt, x_sample, hist_p, hist_s, row(norm_mix_g[0]), pw,
                                             row(pool_scale[0]), route_w(0))
    x2 = _moe_layer(x1, routed, row(norm_ffn_g[0]), moe_w_up, moe_b_up, moe_w_down, moe_b_down,
                    row(norm_final_g), layer=0, final_shapes=None)

    ar, ai, wb, wc = _ssm_weights(ssm_a_re[0], ssm_a_im[0], ssm_log_dt[0], ssm_b_re[0], ssm_b_im[0],
                                  ssm_c_re[0], ssm_c_im[0])
    wglu = jnp.concatenate([ssm_glu_w[0], ssm_glu_gate[0]], axis=1).astype(BF16)
    g_groups, p_state = ssm_a_re.shape[1:]
    sc = g_groups * p_state
    zero_state = jnp.zeros((bp, sc), F32)
    x3, re_p, im_p, re_s, im_s, *routed = _ssm_call(
        x2, row(norm_mix_g[1]), ar, ai, wb, wc, row(ssm_d[0]), wglu, zero_state, zero_state,
        state_ssm_re[0].reshape(bs, sc), state_ssm_im[0].reshape(bs, sc), route_w(1), n_p=n_p)
    y_p, y_s = _moe_layer(x3, routed, row(norm_ffn_g[1]), moe_w_up, moe_b_up, moe_w_down, moe_b_down,
                          row(norm_final_g), layer=1, final_shapes=((bp, lp), (bs, ls)))

    st = lambda a, b: a.reshape(1, b, g_groups, p_state)
    return (y_p, y_s,
            _batch_major(pool_p, bp)[None], _batch_major(pool_s, bs)[None],
            st(re_p, bp), st(im_p, bp), st(re_s, bs), st(im_s, bs))
```

```python
import functools
import math

import jax
import jax.numpy as jnp
from jax import lax
from jax.experimental import pallas as pl
from jax.experimental.pallas import tpu as pltpu

F32 = jnp.float32
BF16 = jnp.bfloat16

POOL_WINDOWS = (2, 4, 8, 16)
POOL_HIST = max(POOL_WINDOWS) - 1
SSM_GROUP_CH = 16
SSM_STATE = 64
TOP_K = 4
SWIGLU_LIMIT = 7.0
SWIGLU_ALPHA = 1.702
RMS_EPS = 1e-5
PAST_LEN = 1024

SUBLANES = 8
LANES = 128
MXU_DIM = 256
VMEM_LIMIT = 56 * 1024 * 1024

MIX_ROWS = 512
TOKEN_BLOCK = 512
FFN_TILE = 512
FFN_TAIL_PARTS = 4
PERM_CHUNK = 128
SSM_SLAB = MXU_DIM


def _rms(x, g):
    return x * lax.rsqrt(jnp.mean(x * x, axis=-1, keepdims=True) + RMS_EPS) * g


def _params(**kw):
    return pltpu.CompilerParams(dimension_semantics=("arbitrary",), vmem_limit_bytes=VMEM_LIMIT, **kw)


def _slab_pitch(tc):
    return tc + SUBLANES if (tc // SUBLANES) % 2 == 0 else tc


def _load_time_major(x_ref, slab_ref, xt_ref):
    bt, tc, d = x_ref.shape
    pitch = _slab_pitch(tc)
    for k in range(d // LANES):
        lanes = slice(k * LANES, (k + 1) * LANES)
        for b in range(bt):
            slab_ref[k, b * pitch:b * pitch + tc, :] = x_ref[b, :, lanes]
        for t in range(tc):
            for b0 in range(0, bt, SUBLANES):
                xt_ref[t * bt + b0:t * bt + b0 + SUBLANES, lanes] = (
                    slab_ref[k, pl.ds(b0 * pitch + t, SUBLANES, stride=pitch), :])


def _store_batch_major(y, slab_ref, o_ref):
    bt, tc, d = o_ref.shape
    for k in range(d // LANES):
        lanes = slice(k * LANES, (k + 1) * LANES)
        slab_ref[k, 0:tc * bt, :] = y[:, lanes]
        for b in range(bt):
            for t0 in range(0, tc, SUBLANES):
                o_ref[b, t0:t0 + SUBLANES, lanes] = slab_ref[k, pl.ds(t0 * bt + b, SUBLANES, stride=bt), :]


def _pool_phase(chunk, x_ref, slab_ref, xt_ref, g_ref, w_ref, scale_ref, o_ref, hist_ref, hist_out_ref, ext_ref,
                route, *, start_pos):
    bt, tc, d = x_ref.shape
    rc = bt * tc
    hr = POOL_HIST * bt
    gw = d // len(POOL_WINDOWS)

    @pl.when(chunk == 0)
    def _():
        ext_ref[0:hr, :] = hist_ref[...]

    _load_time_major(x_ref, slab_ref, xt_ref)
    x = xt_ref[...]
    u = _rms(x, g_ref[...])
    ext_ref[hr:hr + rc, :] = u
    row = lax.broadcasted_iota(jnp.int32, (rc, 1), 0)
    pos = start_pos + chunk * tc + row // bt
    outs = []
    for gi, w in enumerate(POOL_WINDOWS):
        c0 = gi * gw
        s = ext_ref[hr:hr + rc, c0:c0 + gw]
        for j in range(1, w):
            s = s + ext_ref[hr - j * bt:hr - j * bt + rc, c0:c0 + gw]
        cnt = jnp.minimum(w, pos + 1).astype(F32)
        mixed = s / cnt - u[:, c0:c0 + gw]
        outs.append(jnp.dot(mixed.astype(BF16), w_ref[gi], preferred_element_type=F32))
    y = jnp.concatenate(outs, axis=-1) * scale_ref[...]
    out = x + y
    o_ref[...] = out
    _route_rows(out, *route)
    ext_ref[0:hr, :] = ext_ref[rc:rc + hr, :]
    hist_out_ref[...] = ext_ref[0:hr, :]


def _pool_kernel(xp_ref, xs_ref, hist_p_ref, hist_s_ref, g_ref, w_ref, scale_ref, gffn_ref, wr_ref, br_ref,
                 o_ref, hout_p_ref, hout_s_ref, gate_ref, lpos_ref, cnt_ref,
                 ext_p_ref, ext_s_ref, slab_ref, xt_ref, *, n_chunks_p):
    i = pl.program_id(0)
    route = (gffn_ref, wr_ref, br_ref, gate_ref, lpos_ref, cnt_ref)

    @pl.when(i < n_chunks_p)
    def _():
        _pool_phase(i, xp_ref, slab_ref, xt_ref, g_ref, w_ref, scale_ref, o_ref, hist_p_ref, hout_p_ref,
                    ext_p_ref, route, start_pos=0)

    @pl.when(i >= n_chunks_p)
    def _():
        _pool_phase(i - n_chunks_p, xs_ref, slab_ref, xt_ref, g_ref, w_ref, scale_ref, o_ref, hist_s_ref,
                    hout_s_ref, ext_s_ref, route, start_pos=PAST_LEN)


def _pool_call(x_p, x_s, hist_p, hist_s, g, w_bf, scale, route_w):
    bt_p, l_p, d = x_p.shape
    bt_s, l_s, _ = x_s.shape
    rc = MIX_ROWS
    tc_p, tc_s = rc // bt_p, rc // bt_s
    n_p, n = bt_p * l_p, bt_p * l_p + bt_s * l_s
    ncp = n_p // rc
    hr_p, hr_s = POOL_HIST * bt_p, POOL_HIST * bt_s
    assert l_p % tc_p == 0 and l_s % tc_s == 0 and rc > max(hr_p, hr_s)
    assert tc_p % SUBLANES == 0 and tc_s % SUBLANES == 0 and bt_p % SUBLANES == 0 and bt_s % SUBLANES == 0
    slab_rows = max(bt_p * _slab_pitch(tc_p), bt_s * _slab_pitch(tc_s))
    kern = functools.partial(_pool_kernel, n_chunks_p=ncp)
    const2 = lambda i: (0, 0)
    r_in, r_out, r_shape = _route_specs(n, route_w)
    return pl.pallas_call(
        kern,
        grid=(n // rc,),
        in_specs=[
            pl.BlockSpec((bt_p, tc_p, d), lambda i: (0, jnp.minimum(i, ncp - 1), 0)),
            pl.BlockSpec((bt_s, tc_s, d), lambda i: (0, jnp.maximum(i - ncp, 0), 0)),
            pl.BlockSpec((hr_p, d), const2),
            pl.BlockSpec((hr_s, d), const2),
            pl.BlockSpec((1, d), const2),
            pl.BlockSpec(w_bf.shape, lambda i: (0, 0, 0)),
            pl.BlockSpec((1, d), const2),
        ] + r_in,
        out_specs=[pl.BlockSpec((rc, d), lambda i: (i, 0)),
                   pl.BlockSpec((hr_p, d), const2), pl.BlockSpec((hr_s, d), const2)] + r_out,
        out_shape=[jax.ShapeDtypeStruct((n, d), F32),
                   jax.ShapeDtypeStruct((hr_p, d), F32), jax.ShapeDtypeStruct((hr_s, d), F32)] + r_shape,
        scratch_shapes=[pltpu.VMEM((hr_p + rc, d), F32), pltpu.VMEM((hr_s + rc, d), F32),
                        pltpu.VMEM((d // LANES, slab_rows, LANES), F32), pltpu.VMEM((rc, d), F32)],
        compiler_params=_params(),
        name="pool_mixer",
    )(x_p, x_s, hist_p, hist_s, g, w_bf, scale, *route_w)


def _gelu_tanh(x):
    return 0.5 * x * (1.0 + jnp.tanh(math.sqrt(2.0 / math.pi) * (x + 0.044715 * (x * x * x))))


def _ssm_phase(chunk, x_ref, g_ref, ar_ref, ai_ref, wb_ref, wc_ref, dsk_ref, wglu_ref, o_ref, bu_ref, y_ref, route,
               h0r_ref, h0i_ref, hr_out_ref, hi_out_ref, hre_ref, him_ref):
    bt = hre_ref.shape[0]
    tc = x_ref.shape[0] // bt
    d = x_ref.shape[1]
    n_slab = d // SSM_SLAB
    sw = hre_ref.shape[1] // n_slab

    @pl.when(chunk == 0)
    def _():
        hre_ref[...] = h0r_ref[...]
        him_ref[...] = h0i_ref[...]

    x = x_ref[...]
    u = _rms(x, g_ref[...])
    ub = u.astype(BF16)
    for s in range(n_slab):
        bu_ref[s] = jnp.dot(ub[:, s * SSM_SLAB:(s + 1) * SSM_SLAB], wb_ref[s], preferred_element_type=F32)
    for s in range(n_slab):
        ar = ar_ref[s]
        ai = ai_ref[s]
        cols = slice(s * sw, (s + 1) * sw)
        for b0 in range(0, bt, SUBLANES):
            h_re = hre_ref[b0:b0 + SUBLANES, cols]
            h_im = him_ref[b0:b0 + SUBLANES, cols]
            for t in range(tc):
                rows = slice(t * bt + b0, t * bt + b0 + SUBLANES)
                h_re, h_im = (ar * h_re - ai * h_im + bu_ref[s, rows, 0:sw],
                              ar * h_im + ai * h_re + bu_ref[s, rows, sw:2 * sw])
                bu_ref[s, rows, 0:sw] = h_re
                bu_ref[s, rows, sw:2 * sw] = h_im
            hre_ref[b0:b0 + SUBLANES, cols] = h_re
            him_ref[b0:b0 + SUBLANES, cols] = h_im
    for s in range(n_slab):
        y_ref[:, s * SSM_SLAB:(s + 1) * SSM_SLAB] = jnp.dot(
            bu_ref[s].astype(BF16), wc_ref[s], preferred_element_type=F32)
    y = y_ref[...] + dsk_ref[...] * u
    gl = _gelu_tanh(y).astype(BF16)
    z = jnp.dot(gl, wglu_ref[...], preferred_element_type=F32)
    out = x + z[:, :d] * jax.nn.sigmoid(z[:, d:])
    o_ref[...] = out
    _route_rows(out, *route)
    hr_out_ref[...] = hre_ref[...]
    hi_out_ref[...] = him_ref[...]


def _ssm_kernel(x_ref, g_ref, ar_ref, ai_ref, wb_ref, wc_ref, dsk_ref, wglu_ref,
                h0r_p_ref, h0i_p_ref, h0r_s_ref, h0i_s_ref, gffn_ref, wr_ref, br_ref,
                o_ref, hr_p_ref, hi_p_ref, hr_s_ref, hi_s_ref, gate_ref, lpos_ref, cnt_ref,
                bu_ref, y_ref, hre_p_ref, him_p_ref, hre_s_ref, him_s_ref, *, n_chunks_p):
    i = pl.program_id(0)
    route = (gffn_ref, wr_ref, br_ref, gate_ref, lpos_ref, cnt_ref)
    shared = (x_ref, g_ref, ar_ref, ai_ref, wb_ref, wc_ref, dsk_ref, wglu_ref, o_ref, bu_ref, y_ref, route)

    @pl.when(i < n_chunks_p)
    def _():
        _ssm_phase(i, *shared, h0r_p_ref, h0i_p_ref, hr_p_ref, hi_p_ref, hre_p_ref, him_p_ref)

    @pl.when(i >= n_chunks_p)
    def _():
        _ssm_phase(i - n_chunks_p, *shared, h0r_s_ref, h0i_s_ref, hr_s_ref, hi_s_ref, hre_s_ref, him_s_ref)


def _ssm_call(x_all, g, ar, ai, wb, wc, dsk, wglu, h0r_p, h0i_p, h0r_s, h0i_s, route_w, *, n_p):
    n, d = x_all.shape
    rc = MIX_ROWS
    bt_p, sc = h0r_p.shape
    bt_s = h0r_s.shape[0]
    assert n_p % rc == 0 and n % rc == 0 and bt_p % SUBLANES == 0 and bt_s % SUBLANES == 0
    assert rc % bt_p == 0 and rc % bt_s == 0
    n_slab = d // SSM_SLAB
    const2 = lambda i: (0, 0)
    const3 = lambda i: (0, 0, 0)
    state = lambda bt: pl.BlockSpec((bt, sc), const2)
    state_shape = lambda bt: jax.ShapeDtypeStruct((bt, sc), F32)
    r_in, r_out, r_shape = _route_specs(n, route_w)
    return pl.pallas_call(
        functools.partial(_ssm_kernel, n_chunks_p=n_p // rc),
        grid=(n // rc,),
        in_specs=[
            pl.BlockSpec((rc, d), lambda i: (i, 0)),
            pl.BlockSpec((1, d), const2),
            pl.BlockSpec(ar.shape, const3),
            pl.BlockSpec(ai.shape, const3),
            pl.BlockSpec(wb.shape, const3, pipeline_mode=pl.Buffered(1)),
            pl.BlockSpec(wc.shape, const3, pipeline_mode=pl.Buffered(1)),
            pl.BlockSpec((1, d), const2),
            pl.BlockSpec(wglu.shape, const2, pipeline_mode=pl.Buffered(1)),
            state(bt_p), state(bt_p), state(bt_s), state(bt_s),
        ] + r_in,
        out_specs=[pl.BlockSpec((rc, d), lambda i: (i, 0)),
                   state(bt_p), state(bt_p), state(bt_s), state(bt_s)] + r_out,
        out_shape=[jax.ShapeDtypeStruct((n, d), F32),
                   state_shape(bt_p), state_shape(bt_p), state_shape(bt_s), state_shape(bt_s)] + r_shape,
        scratch_shapes=[pltpu.VMEM((n_slab, rc, 2 * sc // n_slab), F32), pltpu.VMEM((rc, d), F32),
                        pltpu.VMEM((bt_p, sc), F32), pltpu.VMEM((bt_p, sc), F32),
                        pltpu.VMEM((bt_s, sc), F32), pltpu.VMEM((bt_s, sc), F32)],
        compiler_params=_params(),
        name="s5_mixer",
    )(x_all, g, ar, ai, wb, wc, dsk, wglu, h0r_p, h0i_p, h0r_s, h0i_s, *route_w)


def _ssm_weights(a_re, a_im, log_dt, b_re, b_im, c_re, c_im):
    g, p = a_re.shape
    h = b_re.shape[2]
    gs = SSM_SLAB // h
    n_slab = g // gs
    dt = jnp.exp(log_dt)[:, None]
    mag = jnp.exp(a_re * dt)
    abr = mag * jnp.cos(a_im * dt)
    abi = mag * jnp.sin(a_im * dt)
    den = a_re * a_re + a_im * a_im
    qr = ((abr - 1.0) * a_re + abi * a_im) / den
    qi = (abi * a_re - (abr - 1.0) * a_im) / den
    bbr = qr[..., None] * b_re - qi[..., None] * b_im
    bbi = qr[..., None] * b_im + qi[..., None] * b_re
    eye = jnp.eye(gs, dtype=F32)

    def in_blockdiag(m):
        m = m.reshape(n_slab, gs, p, h).transpose(0, 1, 3, 2)
        return jnp.einsum("ab,sahp->sahbp", eye, m).reshape(n_slab, gs * h, gs * p)

    def out_blockdiag(m):
        m = m.reshape(n_slab, gs, h, p).transpose(0, 1, 3, 2)
        return jnp.einsum("ab,saph->sapbh", eye, m).reshape(n_slab, gs * p, gs * h)

    wb = jnp.concatenate([in_blockdiag(bbr), in_blockdiag(bbi)], axis=-1).astype(BF16)
    wc = jnp.concatenate([out_blockdiag(c_re), out_blockdiag(-c_im)], axis=1).astype(BF16)
    ar = jnp.broadcast_to(abr.reshape(n_slab, 1, gs * p), (n_slab, SUBLANES, gs * p))
    ai = jnp.broadcast_to(abi.reshape(n_slab, 1, gs * p), (n_slab, SUBLANES, gs * p))
    return ar, ai, wb, wc


def _route_rows(rows, g_ref, wr_ref, br_ref, gate_ref, lpos_ref, cnt_ref):
    tb = TOKEN_BLOCK
    for sub in range(rows.shape[0] // tb):
        _route_block(rows[sub * tb:(sub + 1) * tb, :], g_ref[...], wr_ref[...], br_ref[...],
                     gate_ref.at[sub], lpos_ref.at[sub], cnt_ref.at[sub])


def _route_specs(n, route_w):
    g, wr_split, br = route_w
    ne, d = br.shape[0], wr_split.shape[1]
    tb = TOKEN_BLOCK
    nb, rb = n // tb, MIX_ROWS // tb
    const2 = lambda i: (0, 0)
    blk = lambda i: (i, 0, 0)
    in_specs = [pl.BlockSpec((1, d), const2), pl.BlockSpec((2 * ne, d), const2), pl.BlockSpec((ne, 1), const2)]
    out_specs = [pl.BlockSpec((rb, TOP_K, tb), blk), pl.BlockSpec((rb, TOP_K, tb), blk),
                 pl.BlockSpec((rb, ne, LANES), blk)]
    out_shape = [jax.ShapeDtypeStruct((nb, TOP_K, tb), F32), jax.ShapeDtypeStruct((nb, TOP_K, tb), jnp.int32),
                 jax.ShapeDtypeStruct((nb, ne, LANES), jnp.int32)]
    return in_specs, out_specs, out_shape


def _route_block(x, g, wr, br, gate_ref, lpos_ref, cnt_ref):
    tb = x.shape[0]
    ne = wr.shape[0] // 2
    u = _rms(x, g)
    u_hi = u.astype(BF16)
    u_lo = (u - u_hi.astype(F32)).astype(BF16)
    nt = (((1,), (1,)), ((), ()))
    by_hi = lax.dot_general(wr, u_hi, nt, preferred_element_type=F32)
    logits = by_hi[:ne] + by_hi[ne:] + lax.dot_general(wr[:ne], u_lo, nt, preferred_element_type=F32) + br
    iota_e = lax.broadcasted_iota(jnp.int32, (ne, tb), 0)
    vals, ids = [], []
    l = logits
    for _ in range(TOP_K):
        m = jnp.max(l, axis=0, keepdims=True)
        idx = jnp.min(jnp.where(l == m, iota_e, ne), axis=0, keepdims=True)
        vals.append(m)
        ids.append(idx)
        l = jnp.where(iota_e == idx, -jnp.inf, l)
    ex = [jnp.exp(v - vals[0]) for v in vals]
    den = ex[0] + ex[1] + ex[2] + ex[3]
    gate_ref[...] = jnp.concatenate([e / den for e in ex], axis=0)

    member = jnp.zeros((ne, tb), F32)
    for idx in ids:
        member = jnp.where(iota_e == idx, 1.0, member)
    tri = (lax.broadcasted_iota(jnp.int32, (tb, tb), 0) <= lax.broadcasted_iota(jnp.int32, (tb, tb), 1))
    csum = jnp.dot(member.astype(BF16), tri.astype(BF16), preferred_element_type=F32)
    cnt = csum[:, tb - 1:tb]
    pcnt = jnp.floor((cnt + (SUBLANES - 1)) / SUBLANES) * SUBLANES
    ltri = (lax.broadcasted_iota(jnp.int32, (ne, ne), 1) < lax.broadcasted_iota(jnp.int32, (ne, ne), 0))
    loff = jnp.dot(ltri.astype(BF16), jnp.broadcast_to(pcnt, (ne, LANES)).astype(BF16),
                   preferred_element_type=F32)[:, 0:1]
    lpos = []
    for idx in ids:
        sel = iota_e == idx
        lpos.append(jnp.sum(jnp.where(sel, csum - 1.0 + loff, 0.0), axis=0, keepdims=True))
    lpos_ref[...] = jnp.concatenate(lpos, axis=0).astype(jnp.int32)
    cnt_ref[...] = jnp.broadcast_to(cnt, (ne, LANES)).astype(jnp.int32)


def _local_rows(ne):
    worst = TOKEN_BLOCK * TOP_K + ne * (SUBLANES - 1)
    return -(-worst // LANES) * LANES


def _run_copies(n8, src_ref, src0, dst_ref, dst0, sem, priority=0):
    top = (TOKEN_BLOCK // SUBLANES).bit_length() - 1
    for bit in range(top, -1, -1):
        size = SUBLANES << bit
        off = (n8 >> (bit + 1)) << (bit + 1 + 3)

        @pl.when(((n8 >> bit) & 1) == 1)
        def _(size=size, off=off):
            pltpu.make_async_copy(
                src_ref.at[pl.ds(pl.multiple_of(src0 + off, SUBLANES), size)],
                dst_ref.at[pl.ds(pl.multiple_of(dst0 + off, SUBLANES), size)], sem).start(priority=priority)


def _wait_rows(n_rows, src_ref, dst_ref, sem):
    n8 = n_rows // SUBLANES
    top = (min(src_ref.shape[0], dst_ref.shape[0]) // SUBLANES).bit_length() - 1
    for bit in range(top, -1, -1):
        size = SUBLANES << bit

        @pl.when(((n8 >> bit) & 1) == 1)
        def _(size=size):
            pltpu.make_async_copy(src_ref.at[pl.ds(0, size)], dst_ref.at[pl.ds(0, size)], sem).wait()


def _onehot_chunk(r0, tb, lp, vals):
    iota_r = lax.broadcasted_iota(jnp.int32, (PERM_CHUNK, tb), 0) + r0
    chunk = jnp.zeros((PERM_CHUNK, tb), F32)
    for k in range(TOP_K):
        chunk = jnp.where(iota_r == lp[k:k + 1, :], vals[k], chunk)
    return chunk.astype(BF16)


def _sort_kernel(pc8_ref, loff_ref, gst_ref, used_ref, gap8_ref, gapst_ref, x_ref, g_ref, lpos_ref, xs_ref,
                 buf0_ref, buf1_ref, buf2_ref, zero_ref, sem, zsem, *, ne):
    b = pl.program_id(0)
    nb = pl.num_programs(0)
    tb = x_ref.shape[0]
    bufs = (buf0_ref, buf1_ref, buf2_ref)
    nbuf = len(bufs)
    lr = buf0_ref.shape[0]

    def drain(blk, live, p):
        _wait_rows(jnp.where(live, used_ref[blk], 0), bufs[p], xs_ref, sem.at[p])

    def start_runs(blk, live, p):
        for e in range(ne):
            j = blk * ne + e
            _run_copies(jnp.where(live, pc8_ref[j], 0), bufs[p], loff_ref[j], xs_ref, gst_ref[j], sem.at[p],
                        priority=e % 2)

    @pl.when(b == 0)
    def _():
        zero_ref[...] = jnp.zeros(zero_ref.shape, zero_ref.dtype)

        def fill(e, c):
            _run_copies(gap8_ref[e], zero_ref, 0, xs_ref, gapst_ref[e], zsem)
            return c

        lax.fori_loop(0, ne, fill, 0)

    def step(p):
        drain(jnp.maximum(b - nbuf, 0), b >= nbuf, p)
        start_runs(jnp.maximum(b - 1, 0), b >= 1, (p - 1) % nbuf)

        ub = _rms(x_ref[...], g_ref[...]).astype(BF16)
        lp = lpos_ref[0]
        for r0 in range(0, lr, PERM_CHUNK):
            bufs[p][r0:r0 + PERM_CHUNK, :] = jnp.dot(
                _onehot_chunk(r0, tb, lp, [1.0] * TOP_K), ub, preferred_element_type=F32)

        @pl.when(b == nb - 1)
        def _():
            start_runs(b, True, p)
            for back in range(nbuf - 1, -1, -1):
                drain(jnp.maximum(b - back, 0), b >= back, (p - back) % nbuf)

            def fill_done(e, c):
                _wait_rows(gap8_ref[e] * SUBLANES, zero_ref, xs_ref, zsem)
                return c

            lax.fori_loop(0, ne, fill_done, 0)

    for p in range(nbuf):
        pl.when(b % nbuf == p)(functools.partial(step, p))


def _sort_call(x, g, lpos, pc8, loff, gst, used, gap8, gapst, *, n_rows_out, ne):
    n, d = x.shape
    tb = TOKEN_BLOCK
    nb = n // tb
    lr = _local_rows(ne)
    gs = pltpu.PrefetchScalarGridSpec(
        num_scalar_prefetch=6,
        grid=(nb,),
        in_specs=[pl.BlockSpec((tb, d), lambda i, *_: (i, 0)), pl.BlockSpec((1, d), lambda i, *_: (0, 0)),
                  pl.BlockSpec((1, TOP_K, tb), lambda i, *_: (i, 0, 0))],
        out_specs=pl.BlockSpec(memory_space=pl.ANY),
        scratch_shapes=[pltpu.VMEM((lr, d), F32)] * 3 + [
            pltpu.VMEM((TOKEN_BLOCK, d), F32),
            pltpu.SemaphoreType.DMA((3,)), pltpu.SemaphoreType.DMA(())],
    )
    return pl.pallas_call(
        functools.partial(_sort_kernel, ne=ne),
        grid_spec=gs,
        out_shape=jax.ShapeDtypeStruct((n_rows_out, d), F32),
        compiler_params=_params(has_side_effects=True),
        name="moe_sort",
    )(pc8, loff, gst, used, gap8, gapst, x, g, lpos)


def _ffn_kernel(t0_ref, nt_ref, short_ref, xs_ref, wup_ref, bup_ref, wdn_ref, bdn_ref, ys_ref,
                wup_bf, wdn_bf, xbuf, ybuf, sem_in, sem_out):
    e = pl.program_id(0)
    ne = pl.num_programs(0)
    tm = xbuf.shape[1]
    f = wdn_ref.shape[0]
    t0 = t0_ref[e]
    nt = nt_ref[e]
    total = t0_ref[ne - 1] + nt_ref[ne - 1]

    def fetch(g, rows):
        return pltpu.make_async_copy(xs_ref.at[pl.ds(pl.multiple_of(g * tm, tm), rows)],
                                     xbuf.at[g % 2, pl.ds(0, rows)], sem_in.at[g % 2])

    def writeback(g, rows):
        return pltpu.make_async_copy(ybuf.at[g % 2, pl.ds(0, rows)],
                                     ys_ref.at[pl.ds(pl.multiple_of(g * tm, tm), rows)], sem_out.at[g % 2])

    def by_size(g, fn):
        for k in range(FFN_TAIL_PARTS):
            pl.when(short_ref[g] == k)(functools.partial(fn, tm - k * (tm // FFN_TAIL_PARTS)))

    @pl.when(e == 0)
    def _():
        by_size(0, lambda rows: fetch(0, rows).start())

    @pl.when(nt > 0)
    def _():
        wup_bf[...] = wup_ref[...].astype(BF16)
        wdn_bf[...] = wdn_ref[...].astype(BF16)

    def tile(j, c):
        g = t0 + j

        def run(rows):
            fetch(g, rows).wait()

            @pl.when(g + 1 < total)
            def _():
                by_size(g + 1, lambda r: fetch(g + 1, r).start())

            @pl.when(g >= 2)
            def _():
                by_size(g - 2, lambda r: writeback(g - 2, r).wait())

            x = xbuf[g % 2, pl.ds(0, rows)].astype(BF16)
            gu = jnp.dot(x, wup_bf[...], preferred_element_type=F32) + bup_ref[...]
            glu = jnp.minimum(gu[:, :f], SWIGLU_LIMIT)
            lin = jnp.clip(gu[:, f:], -SWIGLU_LIMIT, SWIGLU_LIMIT)
            act = glu * jax.nn.sigmoid(SWIGLU_ALPHA * glu) * (lin + 1.0)
            ybuf[g % 2, pl.ds(0, rows)] = (
                jnp.dot(act.astype(BF16), wdn_bf[...], preferred_element_type=F32) + bdn_ref[...])
            writeback(g, rows).start()

        by_size(g, run)
        return c

    lax.fori_loop(0, nt, tile, 0)

    @pl.when(e == ne - 1)
    def _():
        @pl.when(total >= 2)
        def _():
            by_size(total - 2, lambda r: writeback(total - 2, r).wait())

        by_size(total - 1, lambda r: writeback(total - 1, r).wait())


def _ffn_call(xs, w_up, b_up, w_dn, b_dn, tile_start, tiles_e, tile_short, *, layer):
    r, d = xs.shape
    tm = FFN_TILE
    _, ne, _, f2 = w_up.shape
    f = w_dn.shape[2]
    by_expert = lambda e, *_: (layer, e, 0, 0)
    gs = pltpu.PrefetchScalarGridSpec(
        num_scalar_prefetch=3,
        grid=(ne,),
        in_specs=[
            pl.BlockSpec(memory_space=pl.ANY),
            pl.BlockSpec((None, None, d, f2), by_expert),
            pl.BlockSpec((None, None, 1, f2), by_expert),
            pl.BlockSpec((None, None, f, d), by_expert),
            pl.BlockSpec((None, None, 1, d), by_expert),
        ],
        out_specs=pl.BlockSpec(memory_space=pl.ANY),
        scratch_shapes=[pltpu.VMEM((d, f2), BF16), pltpu.VMEM((f, d), BF16),
                        pltpu.VMEM((2, tm, d), F32), pltpu.VMEM((2, tm, d), F32),
                        pltpu.SemaphoreType.DMA((2,)), pltpu.SemaphoreType.DMA((2,))],
    )
    return pl.pallas_call(
        _ffn_kernel,
        grid_spec=gs,
        out_shape=jax.ShapeDtypeStruct((r, d), F32),
        compiler_params=_params(has_side_effects=True),
        name="moe_ffn",
    )(tile_start, tiles_e, tile_short, xs, w_up, b_up[:, :, None, :], w_dn, b_dn[:, :, None, :])


def _combine_kernel(pc8_ref, loff_ref, gst_ref, used_ref, x_ref, lpos_ref, gate_ref, gfin_ref, ys_ref, *rest,
                    ne, final_blocks_p):
    if final_blocks_p is None:
        o_ref, buf0_ref, buf1_ref, buf2_ref, perm_ref, sem = rest
    else:
        yp_ref, ysm_ref, buf0_ref, buf1_ref, buf2_ref, perm_ref, slab_ref, sem = rest
    b = pl.program_id(0)
    nb = pl.num_programs(0)
    tb = x_ref.shape[0]
    bufs = (buf0_ref, buf1_ref, buf2_ref)
    nbuf = len(bufs)
    lr = buf0_ref.shape[0]

    def fetch(blk, live, p):
        for e in range(ne):
            j = blk * ne + e
            _run_copies(jnp.where(live, pc8_ref[j], 0), ys_ref, gst_ref[j], bufs[p], loff_ref[j], sem.at[p],
                        priority=e % 2)

    @pl.when(b == 0)
    def _():
        for buf in bufs:
            buf[...] = jnp.zeros(buf.shape, buf.dtype)
        for ahead in range(nbuf - 1):
            fetch(jnp.minimum(ahead, nb - 1), ahead < nb, ahead)

    def step(p):
        _wait_rows(used_ref[b], ys_ref, bufs[p], sem.at[p])
        fetch(jnp.minimum(b + nbuf - 1, nb - 1), b + nbuf - 1 < nb, (p + nbuf - 1) % nbuf)

        gt = gate_ref[0]
        lp = lpos_ref[0]
        for r0 in range(0, lr, PERM_CHUNK):
            perm_ref[r0:r0 + PERM_CHUNK, :] = _onehot_chunk(r0, tb, lp, [gt[k:k + 1, :] for k in range(TOP_K)])

        row = lax.broadcasted_iota(jnp.int32, (lr, 1), 0)
        ys_local = jnp.where(row < used_ref[b], bufs[p][...], 0.0).astype(BF16)
        out = lax.dot_general(perm_ref[...], ys_local, (((0,), (0,)), ((), ())), preferred_element_type=F32)
        y = x_ref[...] + out
        if final_blocks_p is None:
            o_ref[...] = y
        else:
            y = _rms(y, gfin_ref[...])

            @pl.when(b < final_blocks_p)
            def _():
                _store_batch_major(y, slab_ref, yp_ref)

            @pl.when(b >= final_blocks_p)
            def _():
                _store_batch_major(y, slab_ref, ysm_ref)

    for p in range(nbuf):
        pl.when(b % nbuf == p)(functools.partial(step, p))


def _combine_call(x, lpos, gate, gfin, ys, pc8, loff, gst, used, *, ne, final_shapes):
    n, d = x.shape
    tb = TOKEN_BLOCK
    nb = n // tb
    lr = _local_rows(ne)
    scratch = [pltpu.VMEM((lr, d), F32)] * 3 + [pltpu.VMEM((lr, tb), BF16)]
    if final_shapes is None:
        final_blocks_p = None
        out_specs = pl.BlockSpec((tb, d), lambda i, *_: (i, 0))
        out_shape = jax.ShapeDtypeStruct((n, d), F32)
    else:
        (bt_p, l_p), (bt_s, l_s) = final_shapes
        final_blocks_p = bt_p * l_p // tb
        nbp = final_blocks_p
        assert tb % bt_p == 0 and tb % bt_s == 0 and (tb // bt_p) % SUBLANES == 0 and (tb // bt_s) % SUBLANES == 0
        out_specs = [pl.BlockSpec((bt_p, tb // bt_p, d), lambda i, *_: (0, jnp.minimum(i, nbp - 1), 0)),
                     pl.BlockSpec((bt_s, tb // bt_s, d), lambda i, *_: (0, jnp.maximum(i - nbp, 0), 0))]
        out_shape = [jax.ShapeDtypeStruct((bt_p, l_p, d), F32), jax.ShapeDtypeStruct((bt_s, l_s, d), F32)]
        scratch.append(pltpu.VMEM((d // LANES, tb, LANES), F32))
    gs = pltpu.PrefetchScalarGridSpec(
        num_scalar_prefetch=4,
        grid=(nb,),
        in_specs=[pl.BlockSpec((tb, d), lambda i, *_: (i, 0)),
                  pl.BlockSpec((1, TOP_K, tb), lambda i, *_: (i, 0, 0)),
                  pl.BlockSpec((1, TOP_K, tb), lambda i, *_: (i, 0, 0)),
                  pl.BlockSpec((1, d), lambda i, *_: (0, 0)),
                  pl.BlockSpec(memory_space=pl.ANY)],
        out_specs=out_specs,
        scratch_shapes=scratch + [pltpu.SemaphoreType.DMA((3,))],
    )
    return pl.pallas_call(
        functools.partial(_combine_kernel, ne=ne, final_blocks_p=final_blocks_p),
        grid_spec=gs,
        out_shape=out_shape,
        compiler_params=_params(),
        name="moe_combine",
    )(pc8, loff, gst, used, x, lpos, gate, gfin, ys)


def _moe_layer(x, routed, g_ffn, w_up, b_up, w_dn, b_dn, g_final, *, layer, final_shapes):
    n, d = x.shape
    gate, lpos, cnt = routed
    ne = cnt.shape[1]
    tb, tm = TOKEN_BLOCK, FFN_TILE
    nb = n // tb
    cnt = cnt[:, :, 0]
    pc = (cnt + (SUBLANES - 1)) // SUBLANES * SUBLANES
    loff = jnp.cumsum(pc, axis=1) - pc
    rows_e = jnp.sum(pc, axis=0)
    tiles_e = (rows_e + tm - 1) // tm
    tile_end = jnp.cumsum(tiles_e)
    tile_start = tile_end - tiles_e
    gst = (tile_start * tm)[None, :] + jnp.cumsum(pc, axis=0) - pc
    part = tm // FFN_TAIL_PARTS
    tail = rows_e % tm
    short_e = jnp.where(tail > 0, (tm - tail) // part, 0)
    gap = tiles_e * tm - short_e * part - rows_e
    max_rows = n * TOP_K + nb * ne * (SUBLANES - 1)
    max_tiles = (max_rows + ne * (tm - SUBLANES)) // tm
    t = jnp.arange(max_tiles, dtype=jnp.int32)
    tile_short = jnp.sum(jnp.where((t[:, None] == (tile_end - 1)[None, :]) & (tiles_e > 0)[None, :],
                                   short_e[None, :], 0), axis=1)
    i32 = lambda a: a.astype(jnp.int32).reshape(-1)
    pc8, loff, gst = i32(pc // SUBLANES), i32(loff), i32(gst)
    used = i32(jnp.sum(pc, axis=1))
    xs = _sort_call(x, g_ffn, lpos, pc8, loff, gst, used, i32(gap // SUBLANES), i32(tile_start * tm + rows_e),
                    n_rows_out=max_tiles * tm, ne=ne)
    ys = _ffn_call(xs, w_up, b_up, w_dn, b_dn, i32(tile_start), i32(tiles_e), i32(tile_short), layer=layer)
    return _combine_call(x, lpos, gate, g_final, ys, pc8, loff, gst, used, ne=ne, final_shapes=final_shapes)


def _time_major(a):
    b, l, d = a.shape
    return a.transpose(1, 0, 2).reshape(l * b, d)


def _batch_major(a, b):
    n, d = a.shape
    return a.reshape(n // b, b, d).transpose(1, 0, 2)


def kernel(x_prompt, x_sample, cache_pool, state_ssm_re, state_ssm_im, norm_mix_g, norm_ffn_g, norm_final_g,
           pool_w, pool_scale, ssm_a_re, ssm_a_im, ssm_log_dt, ssm_b_re, ssm_b_im, ssm_c_re, ssm_c_im,
           ssm_d, ssm_glu_w, ssm_glu_gate, router_w, router_b, moe_w_up, moe_b_up, moe_w_down, moe_b_down):
    bp, lp, d = x_prompt.shape
    bs, ls, _ = x_sample.shape
    depth = norm_mix_g.shape[0]
    assert depth == 2 and cache_pool.shape[0] == 1 and state_ssm_re.shape[0] == 1
    n_p = bp * lp
    row = lambda v: v.reshape(1, -1)

    def route_w(i):
        wr_t = router_w[i].T
        wr_hi = wr_t.astype(BF16)
        wr_lo = (wr_t - wr_hi.astype(F32)).astype(BF16)
        return row(norm_ffn_g[i]), jnp.concatenate([wr_hi, wr_lo], axis=0), router_b[i][:, None]

    pw = pool_w[0].astype(BF16)
    hist_p = jnp.zeros((POOL_HIST * bp, d), F32)
    hist_s = _time_major(cache_pool[0])
    x1, pool_p, pool_s, *routed = _pool_call(x_prompt, x_sample, hist_p, hist_s, row(norm_mix_g[0]), pw,
                                             row(pool_scale[0]), route_w(0))
    x2 = _moe_layer(x1, routed, row(norm_ffn_g[0]), moe_w_up, moe_b_up, moe_w_down, moe_b_down,
                    row(norm_final_g), layer=0, final_shapes=None)

    ar, ai, wb, wc = _ssm_weights(ssm_a_re[0], ssm_a_im[0], ssm_log_dt[0], ssm_b_re[0], ssm_b_im[0],
                                  ssm_c_re[0], ssm_c_im[0])
    wglu = jnp.concatenate([ssm_glu_w[0], ssm_glu_gate[0]], axis=1).astype(BF16)
    g_groups, p_state = ssm_a_re.shape[1:]
    sc = g_groups * p_state
    zero_state = jnp.zeros((bp, sc), F32)
    x3, re_p, im_p, re_s, im_s, *routed = _ssm_call(
        x2, row(norm_mix_g[1]), ar, ai, wb, wc, row(ssm_d[0]), wglu, zero_state, zero_state,
        state_ssm_re[0].reshape(bs, sc), state_ssm_im[0].reshape(bs, sc), route_w(1), n_p=n_p)
    y_p, y_s = _moe_layer(x3, routed, row(norm_ffn_g[1]), moe_w_up, moe_b_up, moe_w_down, moe_b_down,
                          row(norm_final_g), layer=1, final_shapes=((bp, lp), (bs, ls)))

    st = lambda a, b: a.reshape(1, b, g_groups, p_state)
    return (y_p, y_s,
            _batch_major(pool_p, bp)[None], _batch_major(pool_s, bs)[None],
            st(re_p, bp), st(im_p, bp), st(re_s, bs), st(im_s, bs))
```

```python
import functools
import math

import jax
import jax.numpy as jnp
from jax import lax
from jax.experimental import pallas as pl
from jax.experimental.pallas import tpu as pltpu

F32 = jnp.float32
BF16 = jnp.bfloat16

POOL_WINDOWS = (2, 4, 8, 16)
POOL_HIST = max(POOL_WINDOWS) - 1
SSM_GROUP_CH = 16
SSM_STATE = 64
TOP_K = 4
SWIGLU_LIMIT = 7.0
SWIGLU_ALPHA = 1.702
RMS_EPS = 1e-5
PAST_LEN = 1024

SUBLANES = 8
LANES = 128
MXU_DIM = 256
VMEM_LIMIT = 56 * 1024 * 1024

MIX_ROWS = 512
TOKEN_BLOCK = 512
FFN_TILE = 512
FFN_TAIL_PARTS = 8
PERM_CHUNK = 128
SSM_SLAB = MXU_DIM


def _rms(x, g):
    return x * lax.rsqrt(jnp.mean(x * x, axis=-1, keepdims=True) + RMS_EPS) * g


def _params(**kw):
    return pltpu.CompilerParams(dimension_semantics=("arbitrary",), vmem_limit_bytes=VMEM_LIMIT, **kw)


def _slab_pitch(tc):
    return tc + SUBLANES if (tc // SUBLANES) % 2 == 0 else tc


def _load_time_major(x_ref, slab_ref, xt_ref):
    bt, tc, d = x_ref.shape
    pitch = _slab_pitch(tc)
    for k in range(d // LANES):
        lanes = slice(k * LANES, (k + 1) * LANES)
        for b in range(bt):
            slab_ref[k, b * pitch:b * pitch + tc, :] = x_ref[b, :, lanes]
        for t in range(tc):
            for b0 in range(0, bt, SUBLANES):
                xt_ref[t * bt + b0:t * bt + b0 + SUBLANES, lanes] = (
                    slab_ref[k, pl.ds(b0 * pitch + t, SUBLANES, stride=pitch), :])


def _store_batch_major(y, slab_ref, o_ref):
    bt, tc, d = o_ref.shape
    for k in range(d // LANES):
        lanes = slice(k * LANES, (k + 1) * LANES)
        slab_ref[k, 0:tc * bt, :] = y[:, lanes]
        for b in range(bt):
            for t0 in range(0, tc, SUBLANES):
                o_ref[b, t0:t0 + SUBLANES, lanes] = slab_ref[k, pl.ds(t0 * bt + b, SUBLANES, stride=bt), :]


def _pool_phase(chunk, x_ref, slab_ref, xt_ref, g_ref, w_ref, scale_ref, o_ref, hist_ref, hist_out_ref, ext_ref,
                route, *, start_pos):
    bt, tc, d = x_ref.shape
    rc = bt * tc
    hr = POOL_HIST * bt
    gw = d // len(POOL_WINDOWS)

    @pl.when(chunk == 0)
    def _():
        ext_ref[0:hr, :] = hist_ref[...]

    _load_time_major(x_ref, slab_ref, xt_ref)
    x = xt_ref[...]
    u = _rms(x, g_ref[...])
    ext_ref[hr:hr + rc, :] = u
    row = lax.broadcasted_iota(jnp.int32, (rc, 1), 0)
    pos = start_pos + chunk * tc + row // bt
    outs = []
    for gi, w in enumerate(POOL_WINDOWS):
        c0 = gi * gw
        s = ext_ref[hr:hr + rc, c0:c0 + gw]
        for j in range(1, w):
            s = s + ext_ref[hr - j * bt:hr - j * bt + rc, c0:c0 + gw]
        cnt = jnp.minimum(w, pos + 1).astype(F32)
        mixed = s / cnt - u[:, c0:c0 + gw]
        outs.append(jnp.dot(mixed.astype(BF16), w_ref[gi], preferred_element_type=F32))
    y = jnp.concatenate(outs, axis=-1) * scale_ref[...]
    out = x + y
    o_ref[...] = out
    _route_rows(out, *route)
    ext_ref[0:hr, :] = ext_ref[rc:rc + hr, :]
    hist_out_ref[...] = ext_ref[0:hr, :]


def _pool_kernel(xp_ref, xs_ref, hist_p_ref, hist_s_ref, g_ref, w_ref, scale_ref, gffn_ref, wr_ref, br_ref,
                 o_ref, hout_p_ref, hout_s_ref, gate_ref, lpos_ref, cnt_ref,
                 ext_p_ref, ext_s_ref, slab_ref, xt_ref, *, n_chunks_p):
    i = pl.program_id(0)
    route = (gffn_ref, wr_ref, br_ref, gate_ref, lpos_ref, cnt_ref)

    @pl.when(i < n_chunks_p)
    def _():
        _pool_phase(i, xp_ref, slab_ref, xt_ref, g_ref, w_ref, scale_ref, o_ref, hist_p_ref, hout_p_ref,
                    ext_p_ref, route, start_pos=0)

    @pl.when(i >= n_chunks_p)
    def _():
        _pool_phase(i - n_chunks_p, xs_ref, slab_ref, xt_ref, g_ref, w_ref, scale_ref, o_ref, hist_s_ref,
                    hout_s_ref, ext_s_ref, route, start_pos=PAST_LEN)


def _pool_call(x_p, x_s, hist_p, hist_s, g, w_bf, scale, route_w):
    bt_p, l_p, d = x_p.shape
    bt_s, l_s, _ = x_s.shape
    rc = MIX_ROWS
    tc_p, tc_s = rc // bt_p, rc // bt_s
    n_p, n = bt_p * l_p, bt_p * l_p + bt_s * l_s
    ncp = n_p // rc
    hr_p, hr_s = POOL_HIST * bt_p, POOL_HIST * bt_s
    assert l_p % tc_p == 0 and l_s % tc_s == 0 and rc > max(hr_p, hr_s)
    assert tc_p % SUBLANES == 0 and tc_s % SUBLANES == 0 and bt_p % SUBLANES == 0 and bt_s % SUBLANES == 0
    slab_rows = max(bt_p * _slab_pitch(tc_p), bt_s * _slab_pitch(tc_s))
    kern = functools.partial(_pool_kernel, n_chunks_p=ncp)
    const2 = lambda i: (0, 0)
    r_in, r_out, r_shape = _route_specs(n, route_w)
    return pl.pallas_call(
        kern,
        grid=(n // rc,),
        in_specs=[
            pl.BlockSpec((bt_p, tc_p, d), lambda i: (0, jnp.minimum(i, ncp - 1), 0)),
            pl.BlockSpec((bt_s, tc_s, d), lambda i: (0, jnp.maximum(i - ncp, 0), 0)),
            pl.BlockSpec((hr_p, d), const2),
            pl.BlockSpec((hr_s, d), const2),
            pl.BlockSpec((1, d), const2),
            pl.BlockSpec(w_bf.shape, lambda i: (0, 0, 0)),
            pl.BlockSpec((1, d), const2),
        ] + r_in,
        out_specs=[pl.BlockSpec((rc, d), lambda i: (i, 0)),
                   pl.BlockSpec((hr_p, d), const2), pl.BlockSpec((hr_s, d), const2)] + r_out,
        out_shape=[jax.ShapeDtypeStruct((n, d), F32),
                   jax.ShapeDtypeStruct((hr_p, d), F32), jax.ShapeDtypeStruct((hr_s, d), F32)] + r_shape,
        scratch_shapes=[pltpu.VMEM((hr_p + rc, d), F32), pltpu.VMEM((hr_s + rc, d), F32),
                        pltpu.VMEM((d // LANES, slab_rows, LANES), F32), pltpu.VMEM((rc, d), F32)],
        compiler_params=_params(),
        name="pool_mixer",
    )(x_p, x_s, hist_p, hist_s, g, w_bf, scale, *route_w)


def _gelu_tanh(x):
    return 0.5 * x * (1.0 + jnp.tanh(math.sqrt(2.0 / math.pi) * (x + 0.044715 * (x * x * x))))


def _ssm_phase(chunk, x_ref, g_ref, ar_ref, ai_ref, wb_ref, wc_ref, dsk_ref, wglu_ref, o_ref, bu_ref, y_ref, route,
               h0r_ref, h0i_ref, hr_out_ref, hi_out_ref, hre_ref, him_ref):
    bt = hre_ref.shape[0]
    tc = x_ref.shape[0] // bt
    d = x_ref.shape[1]
    n_slab = d // SSM_SLAB
    sw = hre_ref.shape[1] // n_slab

    @pl.when(chunk == 0)
    def _():
        hre_ref[...] = h0r_ref[...]
        him_ref[...] = h0i_ref[...]

    x = x_ref[...]
    u = _rms(x, g_ref[...])
    ub = u.astype(BF16)
    for s in range(n_slab):
        bu_ref[s] = jnp.dot(ub[:, s * SSM_SLAB:(s + 1) * SSM_SLAB], wb_ref[s], preferred_element_type=F32)
    for s in range(n_slab):
        ar = ar_ref[s]
        ai = ai_ref[s]
        cols = slice(s * sw, (s + 1) * sw)
        for b0 in range(0, bt, SUBLANES):
            h_re = hre_ref[b0:b0 + SUBLANES, cols]
            h_im = him_ref[b0:b0 + SUBLANES, cols]
            for t in range(tc):
                rows = slice(t * bt + b0, t * bt + b0 + SUBLANES)
                h_re, h_im = (ar * h_re - ai * h_im + bu_ref[s, rows, 0:sw],
                              ar * h_im + ai * h_re + bu_ref[s, rows, sw:2 * sw])
                bu_ref[s, rows, 0:sw] = h_re
                bu_ref[s, rows, sw:2 * sw] = h_im
            hre_ref[b0:b0 + SUBLANES, cols] = h_re
            him_ref[b0:b0 + SUBLANES, cols] = h_im
    for s in range(n_slab):
        y_ref[:, s * SSM_SLAB:(s + 1) * SSM_SLAB] = jnp.dot(
            bu_ref[s].astype(BF16), wc_ref[s], preferred_element_type=F32)
    y = y_ref[...] + dsk_ref[...] * u
    gl = _gelu_tanh(y).astype(BF16)
    z = jnp.dot(gl, wglu_ref[...], preferred_element_type=F32)
    out = x + z[:, :d] * jax.nn.sigmoid(z[:, d:])
    o_ref[...] = out
    _route_rows(out, *route)
    hr_out_ref[...] = hre_ref[...]
    hi_out_ref[...] = him_ref[...]


def _ssm_kernel(x_ref, g_ref, ar_ref, ai_ref, wb_ref, wc_ref, dsk_ref, wglu_ref,
                h0r_p_ref, h0i_p_ref, h0r_s_ref, h0i_s_ref, gffn_ref, wr_ref, br_ref,
                o_ref, hr_p_ref, hi_p_ref, hr_s_ref, hi_s_ref, gate_ref, lpos_ref, cnt_ref,
                bu_ref, y_ref, hre_p_ref, him_p_ref, hre_s_ref, him_s_ref, *, n_chunks_p):
    i = pl.program_id(0)
    route = (gffn_ref, wr_ref, br_ref, gate_ref, lpos_ref, cnt_ref)
    shared = (x_ref, g_ref, ar_ref, ai_ref, wb_ref, wc_ref, dsk_ref, wglu_ref, o_ref, bu_ref, y_ref, route)

    @pl.when(i < n_chunks_p)
    def _():
        _ssm_phase(i, *shared, h0r_p_ref, h0i_p_ref, hr_p_ref, hi_p_ref, hre_p_ref, him_p_ref)

    @pl.when(i >= n_chunks_p)
    def _():
        _ssm_phase(i - n_chunks_p, *shared, h0r_s_ref, h0i_s_ref, hr_s_ref, hi_s_ref, hre_s_ref, him_s_ref)


def _ssm_call(x_all, g, ar, ai, wb, wc, dsk, wglu, h0r_p, h0i_p, h0r_s, h0i_s, route_w, *, n_p):
    n, d = x_all.shape
    rc = MIX_ROWS
    bt_p, sc = h0r_p.shape
    bt_s = h0r_s.shape[0]
    assert n_p % rc == 0 and n % rc == 0 and bt_p % SUBLANES == 0 and bt_s % SUBLANES == 0
    assert rc % bt_p == 0 and rc % bt_s == 0
    n_slab = d // SSM_SLAB
    const2 = lambda i: (0, 0)
    const3 = lambda i: (0, 0, 0)
    state = lambda bt: pl.BlockSpec((bt, sc), const2)
    state_shape = lambda bt: jax.ShapeDtypeStruct((bt, sc), F32)
    r_in, r_out, r_shape = _route_specs(n, route_w)
    return pl.pallas_call(
        functools.partial(_ssm_kernel, n_chunks_p=n_p // rc),
        grid=(n // rc,),
        in_specs=[
            pl.BlockSpec((rc, d), lambda i: (i, 0)),
            pl.BlockSpec((1, d), const2),
            pl.BlockSpec(ar.shape, const3),
            pl.BlockSpec(ai.shape, const3),
            pl.BlockSpec(wb.shape, const3, pipeline_mode=pl.Buffered(1)),
            pl.BlockSpec(wc.shape, const3, pipeline_mode=pl.Buffered(1)),
            pl.BlockSpec((1, d), const2),
            pl.BlockSpec(wglu.shape, const2, pipeline_mode=pl.Buffered(1)),
            state(bt_p), state(bt_p), state(bt_s), state(bt_s),
        ] + r_in,
        out_specs=[pl.BlockSpec((rc, d), lambda i: (i, 0)),
                   state(bt_p), state(bt_p), state(bt_s), state(bt_s)] + r_out,
        out_shape=[jax.ShapeDtypeStruct((n, d), F32),
                   state_shape(bt_p), state_shape(bt_p), state_shape(bt_s), state_shape(bt_s)] + r_shape,
        scratch_shapes=[pltpu.VMEM((n_slab, rc, 2 * sc // n_slab), F32), pltpu.VMEM((rc, d), F32),
                        pltpu.VMEM((bt_p, sc), F32), pltpu.VMEM((bt_p, sc), F32),
                        pltpu.VMEM((bt_s, sc), F32), pltpu.VMEM((bt_s, sc), F32)],
        compiler_params=_params(),
        name="s5_mixer",
    )(x_all, g, ar, ai, wb, wc, dsk, wglu, h0r_p, h0i_p, h0r_s, h0i_s, *route_w)


def _ssm_weights(a_re, a_im, log_dt, b_re, b_im, c_re, c_im):
    g, p = a_re.shape
    h = b_re.shape[2]
    gs = SSM_SLAB // h
    n_slab = g // gs
    dt = jnp.exp(log_dt)[:, None]
    mag = jnp.exp(a_re * dt)
    abr = mag * jnp.cos(a_im * dt)
    abi = mag * jnp.sin(a_im * dt)
    den = a_re * a_re + a_im * a_im
    qr = ((abr - 1.0) * a_re + abi * a_im) / den
    qi = (abi * a_re - (abr - 1.0) * a_im) / den
    bbr = qr[..., None] * b_re - qi[..., None] * b_im
    bbi = qr[..., None] * b_im + qi[..., None] * b_re
    eye = jnp.eye(gs, dtype=F32)

    def in_blockdiag(m):
        m = m.reshape(n_slab, gs, p, h).transpose(0, 1, 3, 2)
        return jnp.einsum("ab,sahp->sahbp", eye, m).reshape(n_slab, gs * h, gs * p)

    def out_blockdiag(m):
        m = m.reshape(n_slab, gs, h, p).transpose(0, 1, 3, 2)
        return jnp.einsum("ab,saph->sapbh", eye, m).reshape(n_slab, gs * p, gs * h)

    wb = jnp.concatenate([in_blockdiag(bbr), in_blockdiag(bbi)], axis=-1).astype(BF16)
    wc = jnp.concatenate([out_blockdiag(c_re), out_blockdiag(-c_im)], axis=1).astype(BF16)
    ar = jnp.broadcast_to(abr.reshape(n_slab, 1, gs * p), (n_slab, SUBLANES, gs * p))
    ai = jnp.broadcast_to(abi.reshape(n_slab, 1, gs * p), (n_slab, SUBLANES, gs * p))
    return ar, ai, wb, wc


def _route_rows(rows, g_ref, wr_ref, br_ref, gate_ref, lpos_ref, cnt_ref):
    tb = TOKEN_BLOCK
    for sub in range(rows.shape[0] // tb):
        _route_block(rows[sub * tb:(sub + 1) * tb, :], g_ref[...], wr_ref[...], br_ref[...],
                     gate_ref.at[sub], lpos_ref.at[sub], cnt_ref.at[sub])


def _route_specs(n, route_w):
    g, wr_split, br = route_w
    ne, d = br.shape[0], wr_split.shape[1]
    tb = TOKEN_BLOCK
    nb, rb = n // tb, MIX_ROWS // tb
    const2 = lambda i: (0, 0)
    blk = lambda i: (i, 0, 0)
    in_specs = [pl.BlockSpec((1, d), const2), pl.BlockSpec((2 * ne, d), const2), pl.BlockSpec((ne, 1), const2)]
    out_specs = [pl.BlockSpec((rb, TOP_K, tb), blk), pl.BlockSpec((rb, TOP_K, tb), blk),
                 pl.BlockSpec((rb, ne, LANES), blk)]
    out_shape = [jax.ShapeDtypeStruct((nb, TOP_K, tb), F32), jax.ShapeDtypeStruct((nb, TOP_K, tb), jnp.int32),
                 jax.ShapeDtypeStruct((nb, ne, LANES), jnp.int32)]
    return in_specs, out_specs, out_shape


def _route_block(x, g, wr, br, gate_ref, lpos_ref, cnt_ref):
    tb = x.shape[0]
    ne = wr.shape[0] // 2
    u = _rms(x, g)
    u_hi = u.astype(BF16)
    u_lo = (u - u_hi.astype(F32)).astype(BF16)
    nt = (((1,), (1,)), ((), ()))
    by_hi = lax.dot_general(wr, u_hi, nt, preferred_element_type=F32)
    logits = by_hi[:ne] + by_hi[ne:] + lax.dot_general(wr[:ne], u_lo, nt, preferred_element_type=F32) + br
    iota_e = lax.broadcasted_iota(jnp.int32, (ne, tb), 0)
    vals, ids = [], []
    l = logits
    for _ in range(TOP_K):
        m = jnp.max(l, axis=0, keepdims=True)
        idx = jnp.min(jnp.where(l == m, iota_e, ne), axis=0, keepdims=True)
        vals.append(m)
        ids.append(idx)
        l = jnp.where(iota_e == idx, -jnp.inf, l)
    ex = [jnp.exp(v - vals[0]) for v in vals]
    den = ex[0] + ex[1] + ex[2] + ex[3]
    gate_ref[...] = jnp.concatenate([e / den for e in ex], axis=0)

    member = jnp.zeros((ne, tb), F32)
    for idx in ids:
        member = jnp.where(iota_e == idx, 1.0, member)
    tri = (lax.broadcasted_iota(jnp.int32, (tb, tb), 0) <= lax.broadcasted_iota(jnp.int32, (tb, tb), 1))
    csum = jnp.dot(member.astype(BF16), tri.astype(BF16), preferred_element_type=F32)
    cnt = csum[:, tb - 1:tb]
    pcnt = jnp.floor((cnt + (SUBLANES - 1)) / SUBLANES) * SUBLANES
    ltri = (lax.broadcasted_iota(jnp.int32, (ne, ne), 1) < lax.broadcasted_iota(jnp.int32, (ne, ne), 0))
    loff = jnp.dot(ltri.astype(BF16), jnp.broadcast_to(pcnt, (ne, LANES)).astype(BF16),
                   preferred_element_type=F32)[:, 0:1]
    lpos = []
    for idx in ids:
        sel = iota_e == idx
        lpos.append(jnp.sum(jnp.where(sel, csum - 1.0 + loff, 0.0), axis=0, keepdims=True))
    lpos_ref[...] = jnp.concatenate(lpos, axis=0).astype(jnp.int32)
    cnt_ref[...] = jnp.broadcast_to(cnt, (ne, LANES)).astype(jnp.int32)


def _local_rows(ne):
    worst = TOKEN_BLOCK * TOP_K + ne * (SUBLANES - 1)
    return -(-worst // LANES) * LANES


def _run_copies(n8, src_ref, src0, dst_ref, dst0, sem):
    top = (TOKEN_BLOCK // SUBLANES).bit_length() - 1
    for bit in range(top, -1, -1):
        size = SUBLANES << bit
        off = (n8 >> (bit + 1)) << (bit + 1 + 3)

        @pl.when(((n8 >> bit) & 1) == 1)
        def _(size=size, off=off):
            pltpu.make_async_copy(
                src_ref.at[pl.ds(pl.multiple_of(src0 + off, SUBLANES), size)],
                dst_ref.at[pl.ds(pl.multiple_of(dst0 + off, SUBLANES), size)], sem).start()


def _wait_rows(n_rows, src_ref, dst_ref, sem):
    n8 = n_rows // SUBLANES
    top = (min(src_ref.shape[0], dst_ref.shape[0]) // SUBLANES).bit_length() - 1
    for bit in range(top, -1, -1):
        size = SUBLANES << bit

        @pl.when(((n8 >> bit) & 1) == 1)
        def _(size=size):
            pltpu.make_async_copy(src_ref.at[pl.ds(0, size)], dst_ref.at[pl.ds(0, size)], sem).wait()


def _onehot_chunk(r0, tb, lp, vals):
    iota_r = lax.broadcasted_iota(jnp.int32, (PERM_CHUNK, tb), 0) + r0
    chunk = jnp.zeros((PERM_CHUNK, tb), F32)
    for k in range(TOP_K):
        chunk = jnp.where(iota_r == lp[k:k + 1, :], vals[k], chunk)
    return chunk.astype(BF16)


def _sort_kernel(pc8_ref, loff_ref, gst_ref, used_ref, gap8_ref, gapst_ref, x_ref, g_ref, lpos_ref, xs_ref,
                 buf0_ref, buf1_ref, buf2_ref, zero_ref, sem, zsem, *, ne):
    b = pl.program_id(0)
    nb = pl.num_programs(0)
    tb = x_ref.shape[0]
    bufs = (buf0_ref, buf1_ref, buf2_ref)
    nbuf = len(bufs)
    lr = buf0_ref.shape[0]

    def drain(blk, live, p):
        _wait_rows(jnp.where(live, used_ref[blk], 0), bufs[p], xs_ref, sem.at[p])

    def start_runs(blk, live, p):
        for e in range(ne):
            j = blk * ne + e
            _run_copies(jnp.where(live, pc8_ref[j], 0), bufs[p], loff_ref[j], xs_ref, gst_ref[j], sem.at[p])

    @pl.when(b == 0)
    def _():
        zero_ref[...] = jnp.zeros(zero_ref.shape, zero_ref.dtype)

        def fill(e, c):
            _run_copies(gap8_ref[e], zero_ref, 0, xs_ref, gapst_ref[e], zsem)
            return c

        lax.fori_loop(0, ne, fill, 0)

    def step(p):
        drain(jnp.maximum(b - nbuf, 0), b >= nbuf, p)
        start_runs(jnp.maximum(b - 1, 0), b >= 1, (p - 1) % nbuf)

        ub = _rms(x_ref[...], g_ref[...]).astype(BF16)
        lp = lpos_ref[0]
        for r0 in range(0, lr, PERM_CHUNK):
            bufs[p][r0:r0 + PERM_CHUNK, :] = jnp.dot(
                _onehot_chunk(r0, tb, lp, [1.0] * TOP_K), ub, preferred_element_type=F32)

        @pl.when(b == nb - 1)
        def _():
            start_runs(b, True, p)
            for back in range(nbuf - 1, -1, -1):
                drain(jnp.maximum(b - back, 0), b >= back, (p - back) % nbuf)

            def fill_done(e, c):
                _wait_rows(gap8_ref[e] * SUBLANES, zero_ref, xs_ref, zsem)
                return c

            lax.fori_loop(0, ne, fill_done, 0)

    for p in range(nbuf):
        pl.when(b % nbuf == p)(functools.partial(step, p))


def _sort_call(x, g, lpos, pc8, loff, gst, used, gap8, gapst, *, n_rows_out, ne):
    n, d = x.shape
    tb = TOKEN_BLOCK
    nb = n // tb
    lr = _local_rows(ne)
    gs = pltpu.PrefetchScalarGridSpec(
        num_scalar_prefetch=6,
        grid=(nb,),
        in_specs=[pl.BlockSpec((tb, d), lambda i, *_: (i, 0)), pl.BlockSpec((1, d), lambda i, *_: (0, 0)),
                  pl.BlockSpec((1, TOP_K, tb), lambda i, *_: (i, 0, 0))],
        out_specs=pl.BlockSpec(memory_space=pl.ANY),
        scratch_shapes=[pltpu.VMEM((lr, d), F32)] * 3 + [
            pltpu.VMEM((TOKEN_BLOCK, d), F32),
            pltpu.SemaphoreType.DMA((3,)), pltpu.SemaphoreType.DMA(())],
    )
    return pl.pallas_call(
        functools.partial(_sort_kernel, ne=ne),
        grid_spec=gs,
        out_shape=jax.ShapeDtypeStruct((n_rows_out, d), F32),
        compiler_params=_params(has_side_effects=True),
        name="moe_sort",
    )(pc8, loff, gst, used, gap8, gapst, x, g, lpos)


def _ffn_kernel(t0_ref, nt_ref, short_ref, xs_ref, wup_ref, bup_ref, wdn_ref, bdn_ref, ys_ref,
                wup_bf, wdn_bf, xbuf, ybuf, sem_in, sem_out):
    e = pl.program_id(0)
    ne = pl.num_programs(0)
    tm = xbuf.shape[1]
    f = wdn_ref.shape[0]
    t0 = t0_ref[e]
    nt = nt_ref[e]
    total = t0_ref[ne - 1] + nt_ref[ne - 1]

    def fetch(g, rows):
        return pltpu.make_async_copy(xs_ref.at[pl.ds(pl.multiple_of(g * tm, tm), rows)],
                                     xbuf.at[g % 2, pl.ds(0, rows)], sem_in.at[g % 2])

    def writeback(g, rows):
        return pltpu.make_async_copy(ybuf.at[g % 2, pl.ds(0, rows)],
                                     ys_ref.at[pl.ds(pl.multiple_of(g * tm, tm), rows)], sem_out.at[g % 2])

    def by_size(g, fn):
        for k in range(FFN_TAIL_PARTS):
            pl.when(short_ref[g] == k)(functools.partial(fn, tm - k * (tm // FFN_TAIL_PARTS)))

    @pl.when(e == 0)
    def _():
        by_size(0, lambda rows: fetch(0, rows).start())

    @pl.when(nt > 0)
    def _():
        wup_bf[...] = wup_ref[...].astype(BF16)
        wdn_bf[...] = wdn_ref[...].astype(BF16)

    def tile(j, c):
        g = t0 + j

        def run(rows):
            fetch(g, rows).wait()

            @pl.when(g + 1 < total)
            def _():
                by_size(g + 1, lambda r: fetch(g + 1, r).start())

            @pl.when(g >= 2)
            def _():
                by_size(g - 2, lambda r: writeback(g - 2, r).wait())

            x = xbuf[g % 2, pl.ds(0, rows)].astype(BF16)
            gu = jnp.dot(x, wup_bf[...], preferred_element_type=F32) + bup_ref[...]
            glu = jnp.minimum(gu[:, :f], SWIGLU_LIMIT)
            lin = jnp.clip(gu[:, f:], -SWIGLU_LIMIT, SWIGLU_LIMIT)
            act = glu * jax.nn.sigmoid(SWIGLU_ALPHA * glu) * (lin + 1.0)
            ybuf[g % 2, pl.ds(0, rows)] = (
                jnp.dot(act.astype(BF16), wdn_bf[...], preferred_element_type=F32) + bdn_ref[...])
            writeback(g, rows).start()

        by_size(g, run)
        return c

    lax.fori_loop(0, nt, tile, 0)

    @pl.when(e == ne - 1)
    def _():
        @pl.when(total >= 2)
        def _():
            by_size(total - 2, lambda r: writeback(total - 2, r).wait())

        by_size(total - 1, lambda r: writeback(total - 1, r).wait())


def _ffn_call(xs, w_up, b_up, w_dn, b_dn, tile_start, tiles_e, tile_short, *, layer):
    r, d = xs.shape
    tm = FFN_TILE
    _, ne, _, f2 = w_up.shape
    f = w_dn.shape[2]
    by_expert = lambda e, *_: (layer, e, 0, 0)
    gs = pltpu.PrefetchScalarGridSpec(
        num_scalar_prefetch=3,
        grid=(ne,),
        in_specs=[
            pl.BlockSpec(memory_space=pl.ANY),
            pl.BlockSpec((None, None, d, f2), by_expert),
            pl.BlockSpec((None, None, 1, f2), by_expert),
            pl.BlockSpec((None, None, f, d), by_expert),
            pl.BlockSpec((None, None, 1, d), by_expert),
        ],
        out_specs=pl.BlockSpec(memory_space=pl.ANY),
        scratch_shapes=[pltpu.VMEM((d, f2), BF16), pltpu.VMEM((f, d), BF16),
                        pltpu.VMEM((2, tm, d), F32), pltpu.VMEM((2, tm, d), F32),
                        pltpu.SemaphoreType.DMA((2,)), pltpu.SemaphoreType.DMA((2,))],
    )
    return pl.pallas_call(
        _ffn_kernel,
        grid_spec=gs,
        out_shape=jax.ShapeDtypeStruct((r, d), F32),
        compiler_params=_params(has_side_effects=True),
        name="moe_ffn",
    )(tile_start, tiles_e, tile_short, xs, w_up, b_up[:, :, None, :], w_dn, b_dn[:, :, None, :])


def _combine_kernel(pc8_ref, loff_ref, gst_ref, used_ref, x_ref, lpos_ref, gate_ref, gfin_ref, ys_ref, *rest,
                    ne, final_blocks_p):
    if final_blocks_p is None:
        o_ref, buf0_ref, buf1_ref, buf2_ref, perm_ref, sem = rest
    else:
        yp_ref, ysm_ref, buf0_ref, buf1_ref, buf2_ref, perm_ref, slab_ref, sem = rest
    b = pl.program_id(0)
    nb = pl.num_programs(0)
    tb = x_ref.shape[0]
    bufs = (buf0_ref, buf1_ref, buf2_ref)
    nbuf = len(bufs)
    lr = buf0_ref.shape[0]

    def fetch(blk, live, p):
        for e in range(ne):
            j = blk * ne + e
            _run_copies(jnp.where(live, pc8_ref[j], 0), ys_ref, gst_ref[j], bufs[p], loff_ref[j], sem.at[p])

    @pl.when(b == 0)
    def _():
        for buf in bufs:
            buf[...] = jnp.zeros(buf.shape, buf.dtype)
        for ahead in range(nbuf - 1):
            fetch(jnp.minimum(ahead, nb - 1), ahead < nb, ahead)

    def step(p):
        _wait_rows(used_ref[b], ys_ref, bufs[p], sem.at[p])
        fetch(jnp.minimum(b + nbuf - 1, nb - 1), b + nbuf - 1 < nb, (p + nbuf - 1) % nbuf)

        gt = gate_ref[0]
        lp = lpos_ref[0]
        for r0 in range(0, lr, PERM_CHUNK):
            perm_ref[r0:r0 + PERM_CHUNK, :] = _onehot_chunk(r0, tb, lp, [gt[k:k + 1, :] for k in range(TOP_K)])

        row = lax.broadcasted_iota(jnp.int32, (lr, 1), 0)
        ys_local = jnp.where(row < used_ref[b], bufs[p][...], 0.0).astype(BF16)
        out = lax.dot_general(perm_ref[...], ys_local, (((0,), (0,)), ((), ())), preferred_element_type=F32)
        y = x_ref[...] + out
        if final_blocks_p is None:
            o_ref[...] = y
        else:
            y = _rms(y, gfin_ref[...])

            @pl.when(b < final_blocks_p)
            def _():
                _store_batch_major(y, slab_ref, yp_ref)

            @pl.when(b >= final_blocks_p)
            def _():
                _store_batch_major(y, slab_ref, ysm_ref)

    for p in range(nbuf):
        pl.when(b % nbuf == p)(functools.partial(step, p))


def _combine_call(x, lpos, gate, gfin, ys, pc8, loff, gst, used, *, ne, final_shapes):
    n, d = x.shape
    tb = TOKEN_BLOCK
    nb = n // tb
    lr = _local_rows(ne)
    scratch = [pltpu.VMEM((lr, d), F32)] * 3 + [pltpu.VMEM((lr, tb), BF16)]
    if final_shapes is None:
        final_blocks_p = None
        out_specs = pl.BlockSpec((tb, d), lambda i, *_: (i, 0))
        out_shape = jax.ShapeDtypeStruct((n, d), F32)
    else:
        (bt_p, l_p), (bt_s, l_s) = final_shapes
        final_blocks_p = bt_p * l_p // tb
        nbp = final_blocks_p
        assert tb % bt_p == 0 and tb % bt_s == 0 and (tb // bt_p) % SUBLANES == 0 and (tb // bt_s) % SUBLANES == 0
        out_specs = [pl.BlockSpec((bt_p, tb // bt_p, d), lambda i, *_: (0, jnp.minimum(i, nbp - 1), 0)),
                     pl.BlockSpec((bt_s, tb // bt_s, d), lambda i, *_: (0, jnp.maximum(i - nbp, 0), 0))]
        out_shape = [jax.ShapeDtypeStruct((bt_p, l_p, d), F32), jax.ShapeDtypeStruct((bt_s, l_s, d), F32)]
        scratch.append(pltpu.VMEM((d // LANES, tb, LANES), F32))
    gs = pltpu.PrefetchScalarGridSpec(
        num_scalar_prefetch=4,
        grid=(nb,),
        in_specs=[pl.BlockSpec((tb, d), lambda i, *_: (i, 0)),
                  pl.BlockSpec((1, TOP_K, tb), lambda i, *_: (i, 0, 0)),
                  pl.BlockSpec((1, TOP_K, tb), lambda i, *_: (i, 0, 0)),
                  pl.BlockSpec((1, d), lambda i, *_: (0, 0)),
                  pl.BlockSpec(memory_space=pl.ANY)],
        out_specs=out_specs,
        scratch_shapes=scratch + [pltpu.SemaphoreType.DMA((3,))],
    )
    return pl.pallas_call(
        functools.partial(_combine_kernel, ne=ne, final_blocks_p=final_blocks_p),
        grid_spec=gs,
        out_shape=out_shape,
        compiler_params=_params(),
        name="moe_combine",
    )(pc8, loff, gst, used, x, lpos, gate, gfin, ys)


def _moe_layer(x, routed, g_ffn, w_up, b_up, w_dn, b_dn, g_final, *, layer, final_shapes):
    n, d = x.shape
    gate, lpos, cnt = routed
    ne = cnt.shape[1]
    tb, tm = TOKEN_BLOCK, FFN_TILE
    nb = n // tb
    cnt = cnt[:, :, 0]
    pc = (cnt + (SUBLANES - 1)) // SUBLANES * SUBLANES
    loff = jnp.cumsum(pc, axis=1) - pc
    rows_e = jnp.sum(pc, axis=0)
    tiles_e = (rows_e + tm - 1) // tm
    tile_end = jnp.cumsum(tiles_e)
    tile_start = tile_end - tiles_e
    gst = (tile_start * tm)[None, :] + jnp.cumsum(pc, axis=0) - pc
    part = tm // FFN_TAIL_PARTS
    tail = rows_e % tm
    short_e = jnp.where(tail > 0, (tm - tail) // part, 0)
    gap = tiles_e * tm - short_e * part - rows_e
    max_rows = n * TOP_K + nb * ne * (SUBLANES - 1)
    max_tiles = (max_rows + ne * (tm - SUBLANES)) // tm
    t = jnp.arange(max_tiles, dtype=jnp.int32)
    tile_short = jnp.sum(jnp.where((t[:, None] == (tile_end - 1)[None, :]) & (tiles_e > 0)[None, :],
                                   short_e[None, :], 0), axis=1)
    i32 = lambda a: a.astype(jnp.int32).reshape(-1)
    pc8, loff, gst = i32(pc // SUBLANES), i32(loff), i32(gst)
    used = i32(jnp.sum(pc, axis=1))
    xs = _sort_call(x, g_ffn, lpos, pc8, loff, gst, used, i32(gap // SUBLANES), i32(tile_start * tm + rows_e),
                    n_rows_out=max_tiles * tm, ne=ne)
    ys = _ffn_call(xs, w_up, b_up, w_dn, b_dn, i32(tile_start), i32(tiles_e), i32(tile_short), layer=layer)
    return _combine_call(x, lpos, gate, g_final, ys, pc8, loff, gst, used, ne=ne, final_shapes=final_shapes)


def _time_major(a):
    b, l, d = a.shape
    return a.transpose(1, 0, 2).reshape(l * b, d)


def _batch_major(a, b):
    n, d = a.shape
    return a.reshape(n // b, b, d).transpose(1, 0, 2)


def kernel(x_prompt, x_sample, cache_pool, state_ssm_re, state_ssm_im, norm_mix_g, norm_ffn_g, norm_final_g,
           pool_w, pool_scale, ssm_a_re, ssm_a_im, ssm_log_dt, ssm_b_re, ssm_b_im, ssm_c_re, ssm_c_im,
           ssm_d, ssm_glu_w, ssm_glu_gate, router_w, router_b, moe_w_up, moe_b_up, moe_w_down, moe_b_down):
    bp, lp, d = x_prompt.shape
    bs, ls, _ = x_sample.shape
    depth = norm_mix_g.shape[0]
    assert depth == 2 and cache_pool.shape[0] == 1 and state_ssm_re.shape[0] == 1
    n_p = bp * lp
    row = lambda v: v.reshape(1, -1)

    def route_w(i):
        wr_t = router_w[i].T
        wr_hi = wr_t.astype(BF16)
        wr_lo = (wr_t - wr_hi.astype(F32)).astype(BF16)
        return row(norm_ffn_g[i]), jnp.concatenate([wr_hi, wr_lo], axis=0), router_b[i][:, None]

    pw = pool_w[0].astype(BF16)
    hist_p = jnp.zeros((POOL_HIST * bp, d), F32)
    hist_s = _time_major(cache_pool[0])
    x1, pool_p, pool_s, *routed = _pool_call(x_prompt, x_sample, hist_p, hist_s, row(norm_mix_g[0]), pw,
                                             row(pool_scale[0]), route_w(0))
    x2 = _moe_layer(x1, routed, row(norm_ffn_g[0]), moe_w_up, moe_b_up, moe_w_down, moe_b_down,
                    row(norm_final_g), layer=0, final_shapes=None)

    ar, ai, wb, wc = _ssm_weights(ssm_a_re[0], ssm_a_im[0], ssm_log_dt[0], ssm_b_re[0], ssm_b_im[0],
                                  ssm_c_re[0], ssm_c_im[0])
    wglu = jnp.concatenate([ssm_glu_w[0], ssm_glu_gate[0]], axis=1).astype(BF16)
    g_groups, p_state = ssm_a_re.shape[1:]
    sc = g_groups * p_state
    zero_state = jnp.zeros((bp, sc), F32)
    x3, re_p, im_p, re_s, im_s, *routed = _ssm_call(
        x2, row(norm_mix_g[1]), ar, ai, wb, wc, row(ssm_d[0]), wglu, zero_state, zero_state,
        state_ssm_re[0].reshape(bs, sc), state_ssm_im[0].reshape(bs, sc), route_w(1), n_p=n_p)
    y_p, y_s = _moe_layer(x3, routed, row(norm_ffn_g[1]), moe_w_up, moe_b_up, moe_w_down, moe_b_down,
                          row(norm_final_g), layer=1, final_shapes=((bp, lp), (bs, ls)))

    st = lambda a, b: a.reshape(1, b, g_groups, p_state)
    return (y_p, y_s,
            _batch_major(pool_p, bp)[None], _batch_major(pool_s, bs)[None],
            st(re_p, bp), st(im_p, bp), st(re_s, bs), st(im_s, bs))
```

```python
import functools
import math

import jax
import jax.numpy as jnp
from jax import lax
from jax.experimental import pallas as pl
from jax.experimental.pallas import tpu as pltpu

F32 = jnp.float32
BF16 = jnp.bfloat16

POOL_WINDOWS = (2, 4, 8, 16)
POOL_HIST = max(POOL_WINDOWS) - 1
SSM_GROUP_CH = 16
SSM_STATE = 64
TOP_K = 4
SWIGLU_LIMIT = 7.0
SWIGLU_ALPHA = 1.702
RMS_EPS = 1e-5
PAST_LEN = 1024

SUBLANES = 8
LANES = 128
MXU_DIM = 256
VMEM_LIMIT = 56 * 1024 * 1024

MIX_ROWS = 512
TOKEN_BLOCK = 512
FFN_TILE = 512
FFN_TAIL_PARTS = 4
PERM_CHUNK = 128
SSM_SLAB = MXU_DIM


def _rms(x, g):
    return x * lax.rsqrt(jnp.mean(x * x, axis=-1, keepdims=True) + RMS_EPS) * g


def _params(**kw):
    return pltpu.CompilerParams(dimension_semantics=("arbitrary",), vmem_limit_bytes=VMEM_LIMIT, **kw)


def _slab_pitch(tc):
    return tc + SUBLANES if (tc // SUBLANES) % 2 == 0 else tc


def _load_time_major(x_ref, slab_ref, xt_ref):
    bt, tc, d = x_ref.shape
    pitch = _slab_pitch(tc)
    for k in range(d // LANES):
        lanes = slice(k * LANES, (k + 1) * LANES)
        for b in range(bt):
            slab_ref[k, b * pitch:b * pitch + tc, :] = x_ref[b, :, lanes]
        for t in range(tc):
            for b0 in range(0, bt, SUBLANES):
                xt_ref[t * bt + b0:t * bt + b0 + SUBLANES, lanes] = (
                    slab_ref[k, pl.ds(b0 * pitch + t, SUBLANES, stride=pitch), :])


def _store_batch_major(y, slab_ref, o_ref):
    bt, tc, d = o_ref.shape
    for k in range(d // LANES):
        lanes = slice(k * LANES, (k + 1) * LANES)
        slab_ref[k, 0:tc * bt, :] = y[:, lanes]
        for b in range(bt):
            for t0 in range(0, tc, SUBLANES):
                o_ref[b, t0:t0 + SUBLANES, lanes] = slab_ref[k, pl.ds(t0 * bt + b, SUBLANES, stride=bt), :]


def _pool_phase(chunk, x_ref, slab_ref, xt_ref, g_ref, w_ref, scale_ref, o_ref, hist_ref, hist_out_ref, ext_ref,
                route, *, start_pos):
    bt, tc, d = x_ref.shape
    rc = bt * tc
    hr = POOL_HIST * bt
    gw = d // len(POOL_WINDOWS)

    @pl.when(chunk == 0)
    def _():
        ext_ref[0:hr, :] = hist_ref[...]

    _load_time_major(x_ref, slab_ref, xt_ref)
    x = xt_ref[...]
    u = _rms(x, g_ref[...])
    ext_ref[hr:hr + rc, :] = u
    row = lax.broadcasted_iota(jnp.int32, (rc, 1), 0)
    pos = start_pos + chunk * tc + row // bt
    outs = []
    for gi, w in enumerate(POOL_WINDOWS):
        c0 = gi * gw
        s = ext_ref[hr:hr + rc, c0:c0 + gw]
        for j in range(1, w):
            s = s + ext_ref[hr - j * bt:hr - j * bt + rc, c0:c0 + gw]
        cnt = jnp.minimum(w, pos + 1).astype(F32)
        mixed = s / cnt - u[:, c0:c0 + gw]
        outs.append(jnp.dot(mixed.astype(BF16), w_ref[gi], preferred_element_type=F32))
    y = jnp.concatenate(outs, axis=-1) * scale_ref[...]
    out = x + y
    o_ref[...] = out
    _route_rows(out, *route)
    ext_ref[0:hr, :] = ext_ref[rc:rc + hr, :]
    hist_out_ref[...] = ext_ref[0:hr, :]


def _pool_kernel(xp_ref, xs_ref, hist_p_ref, hist_s_ref, g_ref, w_ref, scale_ref, gffn_ref, wr_ref, br_ref,
                 o_ref, hout_p_ref, hout_s_ref, gate_ref, lpos_ref, cnt_ref,
                 ext_p_ref, ext_s_ref, slab_ref, xt_ref, *, n_chunks_p):
    i = pl.program_id(0)
    route = (gffn_ref, wr_ref, br_ref, gate_ref, lpos_ref, cnt_ref)

    @pl.when(i < n_chunks_p)
    def _():
        _pool_phase(i, xp_ref, slab_ref, xt_ref, g_ref, w_ref, scale_ref, o_ref, hist_p_ref, hout_p_ref,
                    ext_p_ref, route, start_pos=0)

    @pl.when(i >= n_chunks_p)
    def _():
        _pool_phase(i - n_chunks_p, xs_ref, slab_ref, xt_ref, g_ref, w_ref, scale_ref, o_ref, hist_s_ref,
                    hout_s_ref, ext_s_ref, route, start_pos=PAST_LEN)


def _pool_call(x_p, x_s, hist_p, hist_s, g, w_bf, scale, route_w):
    bt_p, l_p, d = x_p.shape
    bt_s, l_s, _ = x_s.shape
    rc = MIX_ROWS
    tc_p, tc_s = rc // bt_p, rc // bt_s
    n_p, n = bt_p * l_p, bt_p * l_p + bt_s * l_s
    ncp = n_p // rc
    hr_p, hr_s = POOL_HIST * bt_p, POOL_HIST * bt_s
    assert l_p % tc_p == 0 and l_s % tc_s == 0 and rc > max(hr_p, hr_s)
    assert tc_p % SUBLANES == 0 and tc_s % SUBLANES == 0 and bt_p % SUBLANES == 0 and bt_s % SUBLANES == 0
    slab_rows = max(bt_p * _slab_pitch(tc_p), bt_s * _slab_pitch(tc_s))
    kern = functools.partial(_pool_kernel, n_chunks_p=ncp)
    const2 = lambda i: (0, 0)
    r_in, r_out, r_shape = _route_specs(n, route_w)
    return pl.pallas_call(
        kern,
        grid=(n // rc,),
        in_specs=[
            pl.BlockSpec((bt_p, tc_p, d), lambda i: (0, jnp.minimum(i, ncp - 1), 0)),
            pl.BlockSpec((bt_s, tc_s, d), lambda i: (0, jnp.maximum(i - ncp, 0), 0)),
            pl.BlockSpec((hr_p, d), const2),
            pl.BlockSpec((hr_s, d), const2),
            pl.BlockSpec((1, d), const2),
            pl.BlockSpec(w_bf.shape, lambda i: (0, 0, 0)),
            pl.BlockSpec((1, d), const2),
        ] + r_in,
        out_specs=[pl.BlockSpec((rc, d), lambda i: (i, 0)),
                   pl.BlockSpec((hr_p, d), const2), pl.BlockSpec((hr_s, d), const2)] + r_out,
        out_shape=[jax.ShapeDtypeStruct((n, d), F32),
                   jax.ShapeDtypeStruct((hr_p, d), F32), jax.ShapeDtypeStruct((hr_s, d), F32)] + r_shape,
        scratch_shapes=[pltpu.VMEM((hr_p + rc, d), F32), pltpu.VMEM((hr_s + rc, d), F32),
                        pltpu.VMEM((d // LANES, slab_rows, LANES), F32), pltpu.VMEM((rc, d), F32)],
        compiler_params=_params(),
        name="pool_mixer",
    )(x_p, x_s, hist_p, hist_s, g, w_bf, scale, *route_w)


def _gelu_tanh(x):
    return 0.5 * x * (1.0 + jnp.tanh(math.sqrt(2.0 / math.pi) * (x + 0.044715 * (x * x * x))))


def _ssm_phase(chunk, x_ref, g_ref, ar_ref, ai_ref, wb_ref, wc_ref, dsk_ref, wglu_ref, o_ref, bu_ref, y_ref, route,
               h0r_ref, h0i_ref, hr_out_ref, hi_out_ref, hre_ref, him_ref):
    bt = hre_ref.shape[0]
    tc = x_ref.shape[0] // bt
    d = x_ref.shape[1]
    n_slab = d // SSM_SLAB
    sw = hre_ref.shape[1] // n_slab

    @pl.when(chunk == 0)
    def _():
        hre_ref[...] = h0r_ref[...]
        him_ref[...] = h0i_ref[...]

    x = x_ref[...]
    u = _rms(x, g_ref[...])
    ub = u.astype(BF16)
    for s in range(n_slab):
        bu_ref[s] = jnp.dot(ub[:, s * SSM_SLAB:(s + 1) * SSM_SLAB], wb_ref[s], preferred_element_type=F32)
    for s in range(n_slab):
        ar = ar_ref[s]
        ai = ai_ref[s]
        cols = slice(s * sw, (s + 1) * sw)
        for b0 in range(0, bt, SUBLANES):
            h_re = hre_ref[b0:b0 + SUBLANES, cols]
            h_im = him_ref[b0:b0 + SUBLANES, cols]
            for t in range(tc):
                rows = slice(t * bt + b0, t * bt + b0 + SUBLANES)
                h_re, h_im = (ar * h_re - ai * h_im + bu_ref[s, rows, 0:sw],
                              ar * h_im + ai * h_re + bu_ref[s, rows, sw:2 * sw])
                bu_ref[s, rows, 0:sw] = h_re
                bu_ref[s, rows, sw:2 * sw] = h_im
            hre_ref[b0:b0 + SUBLANES, cols] = h_re
            him_ref[b0:b0 + SUBLANES, cols] = h_im
    for s in range(n_slab):
        y_ref[:, s * SSM_SLAB:(s + 1) * SSM_SLAB] = jnp.dot(
            bu_ref[s].astype(BF16), wc_ref[s], preferred_element_type=F32)
    y = y_ref[...] + dsk_ref[...] * u
    gl = _gelu_tanh(y).astype(BF16)
    z = jnp.dot(gl, wglu_ref[...], preferred_element_type=F32)
    out = x + z[:, :d] * jax.nn.sigmoid(z[:, d:])
    o_ref[...] = out
    _route_rows(out, *route)
    hr_out_ref[...] = hre_ref[...]
    hi_out_ref[...] = him_ref[...]


def _ssm_kernel(x_ref, g_ref, ar_ref, ai_ref, wb_ref, wc_ref, dsk_ref, wglu_ref,
                h0r_p_ref, h0i_p_ref, h0r_s_ref, h0i_s_ref, gffn_ref, wr_ref, br_ref,
                o_ref, hr_p_ref, hi_p_ref, hr_s_ref, hi_s_ref, gate_ref, lpos_ref, cnt_ref,
                bu_ref, y_ref, hre_p_ref, him_p_ref, hre_s_ref, him_s_ref, *, n_chunks_p):
    i = pl.program_id(0)
    route = (gffn_ref, wr_ref, br_ref, gate_ref, lpos_ref, cnt_ref)
    shared = (x_ref, g_ref, ar_ref, ai_ref, wb_ref, wc_ref, dsk_ref, wglu_ref, o_ref, bu_ref, y_ref, route)

    @pl.when(i < n_chunks_p)
    def _():
        _ssm_phase(i, *shared, h0r_p_ref, h0i_p_ref, hr_p_ref, hi_p_ref, hre_p_ref, him_p_ref)

    @pl.when(i >= n_chunks_p)
    def _():
        _ssm_phase(i - n_chunks_p, *shared, h0r_s_ref, h0i_s_ref, hr_s_ref, hi_s_ref, hre_s_ref, him_s_ref)


def _ssm_call(x_all, g, ar, ai, wb, wc, dsk, wglu, h0r_p, h0i_p, h0r_s, h0i_s, route_w, *, n_p):
    n, d = x_all.shape
    rc = MIX_ROWS
    bt_p, sc = h0r_p.shape
    bt_s = h0r_s.shape[0]
    assert n_p % rc == 0 and n % rc == 0 and bt_p % SUBLANES == 0 and bt_s % SUBLANES == 0
    assert rc % bt_p == 0 and rc % bt_s == 0
    n_slab = d // SSM_SLAB
    const2 = lambda i: (0, 0)
    const3 = lambda i: (0, 0, 0)
    state = lambda bt: pl.BlockSpec((bt, sc), const2)
    state_shape = lambda bt: jax.ShapeDtypeStruct((bt, sc), F32)
    r_in, r_out, r_shape = _route_specs(n, route_w)
    return pl.pallas_call(
        functools.partial(_ssm_kernel, n_chunks_p=n_p // rc),
        grid=(n // rc,),
        in_specs=[
            pl.BlockSpec((rc, d), lambda i: (i, 0)),
            pl.BlockSpec((1, d), const2),
            pl.BlockSpec(ar.shape, const3),
            pl.BlockSpec(ai.shape, const3),
            pl.BlockSpec(wb.shape, const3, pipeline_mode=pl.Buffered(1)),
            pl.BlockSpec(wc.shape, const3, pipeline_mode=pl.Buffered(1)),
            pl.BlockSpec((1, d), const2),
            pl.BlockSpec(wglu.shape, const2, pipeline_mode=pl.Buffered(1)),
            state(bt_p), state(bt_p), state(bt_s), state(bt_s),
        ] + r_in,
        out_specs=[pl.BlockSpec((rc, d), lambda i: (i, 0)),
                   state(bt_p), state(bt_p), state(bt_s), state(bt_s)] + r_out,
        out_shape=[jax.ShapeDtypeStruct((n, d), F32),
                   state_shape(bt_p), state_shape(bt_p), state_shape(bt_s), state_shape(bt_s)] + r_shape,
        scratch_shapes=[pltpu.VMEM((n_slab, rc, 2 * sc // n_slab), F32), pltpu.VMEM((rc, d), F32),
                        pltpu.VMEM((bt_p, sc), F32), pltpu.VMEM((bt_p, sc), F32),
                        pltpu.VMEM((bt_s, sc), F32), pltpu.VMEM((bt_s, sc), F32)],
        compiler_params=_params(),
        name="s5_mixer",
    )(x_all, g, ar, ai, wb, wc, dsk, wglu, h0r_p, h0i_p, h0r_s, h0i_s, *route_w)


def _ssm_weights(a_re, a_im, log_dt, b_re, b_im, c_re, c_im):
    g, p = a_re.shape
    h = b_re.shape[2]
    gs = SSM_SLAB // h
    n_slab = g // gs
    dt = jnp.exp(log_dt)[:, None]
    mag = jnp.exp(a_re * dt)
    abr = mag * jnp.cos(a_im * dt)
    abi = mag * jnp.sin(a_im * dt)
    den = a_re * a_re + a_im * a_im
    qr = ((abr - 1.0) * a_re + abi * a_im) / den
    qi = (abi * a_re - (abr - 1.0) * a_im) / den
    bbr = qr[..., None] * b_re - qi[..., None] * b_im
    bbi = qr[..., None] * b_im + qi[..., None] * b_re
    eye = jnp.eye(gs, dtype=F32)

    def in_blockdiag(m):
        m = m.reshape(n_slab, gs, p, h).transpose(0, 1, 3, 2)
        return jnp.einsum("ab,sahp->sahbp", eye, m).reshape(n_slab, gs * h, gs * p)

    def out_blockdiag(m):
        m = m.reshape(n_slab, gs, h, p).transpose(0, 1, 3, 2)
        return jnp.einsum("ab,saph->sapbh", eye, m).reshape(n_slab, gs * p, gs * h)

    wb = jnp.concatenate([in_blockdiag(bbr), in_blockdiag(bbi)], axis=-1).astype(BF16)
    wc = jnp.concatenate([out_blockdiag(c_re), out_blockdiag(-c_im)], axis=1).astype(BF16)
    ar = jnp.broadcast_to(abr.reshape(n_slab, 1, gs * p), (n_slab, SUBLANES, gs * p))
    ai = jnp.broadcast_to(abi.reshape(n_slab, 1, gs * p), (n_slab, SUBLANES, gs * p))
    return ar, ai, wb, wc


def _route_rows(rows, g_ref, wr_ref, br_ref, gate_ref, lpos_ref, cnt_ref):
    tb = TOKEN_BLOCK
    for sub in range(rows.shape[0] // tb):
        _route_block(rows[sub * tb:(sub + 1) * tb, :], g_ref[...], wr_ref[...], br_ref[...],
                     gate_ref.at[sub], lpos_ref.at[sub], cnt_ref.at[sub])


def _route_specs(n, route_w):
    g, wr_split, br = route_w
    ne, d = br.shape[0], wr_split.shape[1]
    tb = TOKEN_BLOCK
    nb, rb = n // tb, MIX_ROWS // tb
    const2 = lambda i: (0, 0)
    blk = lambda i: (i, 0, 0)
    in_specs = [pl.BlockSpec((1, d), const2), pl.BlockSpec((2 * ne, d), const2), pl.BlockSpec((ne, 1), const2)]
    out_specs = [pl.BlockSpec((rb, TOP_K, tb), blk), pl.BlockSpec((rb, TOP_K, tb), blk),
                 pl.BlockSpec((rb, ne, LANES), blk)]
    out_shape = [jax.ShapeDtypeStruct((nb, TOP_K, tb), F32), jax.ShapeDtypeStruct((nb, TOP_K, tb), jnp.int32),
                 jax.ShapeDtypeStruct((nb, ne, LANES), jnp.int32)]
    return in_specs, out_specs, out_shape


def _route_block(x, g, wr, br, gate_ref, lpos_ref, cnt_ref):
    tb = x.shape[0]
    ne = wr.shape[0] // 2
    u = _rms(x, g)
    u_hi = u.astype(BF16)
    u_lo = (u - u_hi.astype(F32)).astype(BF16)
    nt = (((1,), (1,)), ((), ()))
    by_hi = lax.dot_general(wr, u_hi, nt, preferred_element_type=F32)
    logits = by_hi[:ne] + by_hi[ne:] + lax.dot_general(wr[:ne], u_lo, nt, preferred_element_type=F32) + br
    iota_e = lax.broadcasted_iota(jnp.int32, (ne, tb), 0)
    vals, ids = [], []
    l = logits
    for _ in range(TOP_K):
        m = jnp.max(l, axis=0, keepdims=True)
        idx = jnp.min(jnp.where(l == m, iota_e, ne), axis=0, keepdims=True)
        vals.append(m)
        ids.append(idx)
        l = jnp.where(iota_e == idx, -jnp.inf, l)
    ex = [jnp.exp(v - vals[0]) for v in vals]
    den = ex[0] + ex[1] + ex[2] + ex[3]
    gate_ref[...] = jnp.concatenate([e / den for e in ex], axis=0)

    member = jnp.zeros((ne, tb), F32)
    for idx in ids:
        member = jnp.where(iota_e == idx, 1.0, member)
    tri = (lax.broadcasted_iota(jnp.int32, (tb, tb), 0) <= lax.broadcasted_iota(jnp.int32, (tb, tb), 1))
    csum = jnp.dot(member.astype(BF16), tri.astype(BF16), preferred_element_type=F32)
    cnt = csum[:, tb - 1:tb]
    pcnt = jnp.floor((cnt + (SUBLANES - 1)) / SUBLANES) * SUBLANES
    ltri = (lax.broadcasted_iota(jnp.int32, (ne, ne), 1) < lax.broadcasted_iota(jnp.int32, (ne, ne), 0))
    loff = jnp.dot(ltri.astype(BF16), jnp.broadcast_to(pcnt, (ne, LANES)).astype(BF16),
                   preferred_element_type=F32)[:, 0:1]
    lpos = []
    for idx in ids:
        sel = iota_e == idx
        lpos.append(jnp.sum(jnp.where(sel, csum - 1.0 + loff, 0.0), axis=0, keepdims=True))
    lpos_ref[...] = jnp.concatenate(lpos, axis=0).astype(jnp.int32)
    cnt_ref[...] = jnp.broadcast_to(cnt, (ne, LANES)).astype(jnp.int32)


def _local_rows(ne):
    worst = TOKEN_BLOCK * TOP_K + ne * (SUBLANES - 1)
    return -(-worst // LANES) * LANES


def _run_copies(n8, src_ref, src0, dst_ref, dst0, sem):
    top = (TOKEN_BLOCK // SUBLANES).bit_length() - 1
    for bit in range(top, -1, -1):
        size = SUBLANES << bit
        off = (n8 >> (bit + 1)) << (bit + 1 + 3)

        @pl.when(((n8 >> bit) & 1) == 1)
        def _(size=size, off=off):
            pltpu.make_async_copy(
                src_ref.at[pl.ds(pl.multiple_of(src0 + off, SUBLANES), size)],
                dst_ref.at[pl.ds(pl.multiple_of(dst0 + off, SUBLANES), size)], sem).start()


def _wait_rows(n_rows, src_ref, dst_ref, sem):
    n8 = n_rows // SUBLANES
    top = (min(src_ref.shape[0], dst_ref.shape[0]) // SUBLANES).bit_length() - 1
    for bit in range(top, -1, -1):
        size = SUBLANES << bit

        @pl.when(((n8 >> bit) & 1) == 1)
        def _(size=size):
            pltpu.make_async_copy(src_ref.at[pl.ds(0, size)], dst_ref.at[pl.ds(0, size)], sem).wait()


def _onehot_chunk(r0, tb, lp, vals):
    iota_r = lax.broadcasted_iota(jnp.int32, (PERM_CHUNK, tb), 0) + r0
    chunk = jnp.zeros((PERM_CHUNK, tb), F32)
    for k in range(TOP_K):
        chunk = jnp.where(iota_r == lp[k:k + 1, :], vals[k], chunk)
    return chunk.astype(BF16)


def _sort_kernel(pc8_ref, loff_ref, gst_ref, used_ref, gap8_ref, gapst_ref, x_ref, g_ref, lpos_ref, xs_ref,
                 buf0_ref, buf1_ref, buf2_ref, zero_ref, sem, zsem, *, ne):
    b = pl.program_id(0)
    nb = pl.num_programs(0)
    tb = x_ref.shape[0]
    bufs = (buf0_ref, buf1_ref, buf2_ref)
    nbuf = len(bufs)
    lr = buf0_ref.shape[0]

    def drain(blk, live, p):
        _wait_rows(jnp.where(live, used_ref[blk], 0), bufs[p], xs_ref, sem.at[p])

    def start_runs(blk, live, p):
        for e in range(ne):
            j = blk * ne + e
            _run_copies(jnp.where(live, pc8_ref[j], 0), bufs[p], loff_ref[j], xs_ref, gst_ref[j], sem.at[p])

    @pl.when(b == 0)
    def _():
        zero_ref[...] = jnp.zeros(zero_ref.shape, zero_ref.dtype)

        def fill(e, c):
            _run_copies(gap8_ref[e], zero_ref, 0, xs_ref, gapst_ref[e], zsem)
            return c

        lax.fori_loop(0, ne, fill, 0)

    def step(p):
        drain(jnp.maximum(b - nbuf, 0), b >= nbuf, p)
        start_runs(jnp.maximum(b - 1, 0), b >= 1, (p - 1) % nbuf)

        ub = _rms(x_ref[...], g_ref[...]).astype(BF16)
        lp = lpos_ref[0]
        for r0 in range(0, lr, PERM_CHUNK):
            bufs[p][r0:r0 + PERM_CHUNK, :] = jnp.dot(
                _onehot_chunk(r0, tb, lp, [1.0] * TOP_K), ub, preferred_element_type=F32)

        @pl.when(b == nb - 1)
        def _():
            start_runs(b, True, p)
            for back in range(nbuf - 1, -1, -1):
                drain(jnp.maximum(b - back, 0), b >= back, (p - back) % nbuf)

            def fill_done(e, c):
                _wait_rows(gap8_ref[e] * SUBLANES, zero_ref, xs_ref, zsem)
                return c

            lax.fori_loop(0, ne, fill_done, 0)

    for p in range(nbuf):
        pl.when(b % nbuf == p)(functools.partial(step, p))


def _sort_call(x, g, lpos, pc8, loff, gst, used, gap8, gapst, *, n_rows_out, ne):
    n, d = x.shape
    tb = TOKEN_BLOCK
    nb = n // tb
    lr = _local_rows(ne)
    gs = pltpu.PrefetchScalarGridSpec(
        num_scalar_prefetch=6,
        grid=(nb,),
        in_specs=[pl.BlockSpec((tb, d), lambda i, *_: (i, 0)), pl.BlockSpec((1, d), lambda i, *_: (0, 0)),
                  pl.BlockSpec((1, TOP_K, tb), lambda i, *_: (i, 0, 0))],
        out_specs=pl.BlockSpec(memory_space=pl.ANY),
        scratch_shapes=[pltpu.VMEM((lr, d), F32)] * 3 + [
            pltpu.VMEM((TOKEN_BLOCK, d), F32),
            pltpu.SemaphoreType.DMA((3,)), pltpu.SemaphoreType.DMA(())],
    )
    return pl.pallas_call(
        functools.partial(_sort_kernel, ne=ne),
        grid_spec=gs,
        out_shape=jax.ShapeDtypeStruct((n_rows_out, d), F32),
        compiler_params=_params(has_side_effects=True),
        name="moe_sort",
    )(pc8, loff, gst, used, gap8, gapst, x, g, lpos)


def _ffn_kernel(t0_ref, nt_ref, short_ref, xs_ref, wup_ref, bup_ref, wdn_ref, bdn_ref, ys_ref,
                wup_bf, wdn_bf, xbuf, ybuf, sem_in, sem_out):
    e = pl.program_id(0)
    ne = pl.num_programs(0)
    tm = xbuf.shape[1]
    f = wdn_ref.shape[0]
    t0 = t0_ref[e]
    nt = nt_ref[e]
    total = t0_ref[ne - 1] + nt_ref[ne - 1]

    def fetch(g, rows):
        return pltpu.make_async_copy(xs_ref.at[pl.ds(pl.multiple_of(g * tm, tm), rows)],
                                     xbuf.at[g % 2, pl.ds(0, rows)], sem_in.at[g % 2])

    def writeback(g, rows):
        return pltpu.make_async_copy(ybuf.at[g % 2, pl.ds(0, rows)],
                                     ys_ref.at[pl.ds(pl.multiple_of(g * tm, tm), rows)], sem_out.at[g % 2])

    def by_size(g, fn):
        for k in range(FFN_TAIL_PARTS):
            pl.when(short_ref[g] == k)(functools.partial(fn, tm - k * (tm // FFN_TAIL_PARTS)))

    @pl.when(e == 0)
    def _():
        by_size(0, lambda rows: fetch(0, rows).start())

    @pl.when(nt > 0)
    def _():
        wup_bf[...] = wup_ref[...].astype(BF16)
        wdn_bf[...] = wdn_ref[...].astype(BF16)

    def tile(j, c):
        g = t0 + j

        def run(rows):
            fetch(g, rows).wait()

            @pl.when(g + 1 < total)
            def _():
                by_size(g + 1, lambda r: fetch(g + 1, r).start())

            @pl.when(g >= 2)
            def _():
                by_size(g - 2, lambda r: writeback(g - 2, r).wait())

            x = xbuf[g % 2, pl.ds(0, rows)].astype(BF16)
            gu = jnp.dot(x, wup_bf[...], preferred_element_type=F32) + bup_ref[...]
            glu = jnp.minimum(gu[:, :f], SWIGLU_LIMIT)
            lin = jnp.clip(gu[:, f:], -SWIGLU_LIMIT, SWIGLU_LIMIT)
            act = glu * jax.nn.sigmoid(SWIGLU_ALPHA * glu) * (lin + 1.0)
            ybuf[g % 2, pl.ds(0, rows)] = (
                jnp.dot(act.astype(BF16), wdn_bf[...], preferred_element_type=F32) + bdn_ref[...])
            writeback(g, rows).start()

        by_size(g, run)
        return c

    lax.fori_loop(0, nt, tile, 0)

    @pl.when(e == ne - 1)
    def _():
        @pl.when(total >= 2)
        def _():
            by_size(total - 2, lambda r: writeback(total - 2, r).wait())

        by_size(total - 1, lambda r: writeback(total - 1, r).wait())


def _ffn_call(xs, w_up, b_up, w_dn, b_dn, tile_start, tiles_e, tile_short, *, layer):
    r, d = xs.shape
    tm = FFN_TILE
    _, ne, _, f2 = w_up.shape
    f = w_dn.shape[2]
    by_expert = lambda e, *_: (layer, e, 0, 0)
    gs = pltpu.PrefetchScalarGridSpec(
        num_scalar_prefetch=3,
        grid=(ne,),
        in_specs=[
            pl.BlockSpec(memory_space=pl.ANY),
            pl.BlockSpec((None, None, d, f2), by_expert),
            pl.BlockSpec((None, None, 1, f2), by_expert),
            pl.BlockSpec((None, None, f, d), by_expert),
            pl.BlockSpec((None, None, 1, d), by_expert),
        ],
        out_specs=pl.BlockSpec(memory_space=pl.ANY),
        scratch_shapes=[pltpu.VMEM((d, f2), BF16), pltpu.VMEM((f, d), BF16),
                        pltpu.VMEM((2, tm, d), F32), pltpu.VMEM((2, tm, d), F32),
                        pltpu.SemaphoreType.DMA((2,)), pltpu.SemaphoreType.DMA((2,))],
    )
    return pl.pallas_call(
        _ffn_kernel,
        grid_spec=gs,
        out_shape=jax.ShapeDtypeStruct((r, d), F32),
        compiler_params=_params(has_side_effects=True),
        name="moe_ffn",
    )(tile_start, tiles_e, tile_short, xs, w_up, b_up[:, :, None, :], w_dn, b_dn[:, :, None, :])


def _combine_kernel(pc8_ref, loff_ref, gst_ref, used_ref, x_ref, lpos_ref, gate_ref, gfin_ref, ys_ref, *rest,
                    ne, final_blocks_p):
    if final_blocks_p is None:
        o_ref, buf0_ref, buf1_ref, buf2_ref, perm_ref, sem = rest
    else:
        yp_ref, ysm_ref, buf0_ref, buf1_ref, buf2_ref, perm_ref, slab_ref, sem = rest
    b = pl.program_id(0)
    nb = pl.num_programs(0)
    tb = x_ref.shape[0]
    bufs = (buf0_ref, buf1_ref, buf2_ref)
    nbuf = len(bufs)
    lr = buf0_ref.shape[0]

    def fetch(blk, live, p):
        for e in range(ne):
            j = blk * ne + e
            _run_copies(jnp.where(live, pc8_ref[j], 0), ys_ref, gst_ref[j], bufs[p], loff_ref[j], sem.at[p])

    @pl.when(b == 0)
    def _():
        for buf in bufs:
            buf[...] = jnp.zeros(buf.shape, buf.dtype)
        for ahead in range(nbuf - 1):
            fetch(jnp.minimum(ahead, nb - 1), ahead < nb, ahead)

    def step(p):
        _wait_rows(used_ref[b], ys_ref, bufs[p], sem.at[p])
        fetch(jnp.minimum(b + nbuf - 1, nb - 1), b + nbuf - 1 < nb, (p + nbuf - 1) % nbuf)

        gt = gate_ref[0]
        lp = lpos_ref[0]
        for r0 in range(0, lr, PERM_CHUNK):
            perm_ref[r0:r0 + PERM_CHUNK, :] = _onehot_chunk(r0, tb, lp, [gt[k:k + 1, :] for k in range(TOP_K)])

        row = lax.broadcasted_iota(jnp.int32, (lr, 1), 0)
        ys_local = jnp.where(row < used_ref[b], bufs[p][...], 0.0).astype(BF16)
        out = lax.dot_general(perm_ref[...], ys_local, (((0,), (0,)), ((), ())), preferred_element_type=F32)
        y = x_ref[...] + out
        if final_blocks_p is None:
            o_ref[...] = y
        else:
            y = _rms(y, gfin_ref[...])

            @pl.when(b < final_blocks_p)
            def _():
                _store_batch_major(y, slab_ref, yp_ref)

            @pl.when(b >= final_blocks_p)
            def _():
                _store_batch_major(y, slab_ref, ysm_ref)

    for p in range(nbuf):
        pl.when(b % nbuf == p)(functools.partial(step, p))


def _combine_call(x, lpos, gate, gfin, ys, pc8, loff, gst, used, *, ne, final_shapes):
    n, d = x.shape
    tb = TOKEN_BLOCK
    nb = n // tb
    lr = _local_rows(ne)
    scratch = [pltpu.VMEM((lr, d), F32)] * 3 + [pltpu.VMEM((lr, tb), BF16)]
    if final_shapes is None:
        final_blocks_p = None
        out_specs = pl.BlockSpec((tb, d), lambda i, *_: (i, 0))
        out_shape = jax.ShapeDtypeStruct((n, d), F32)
    else:
        (bt_p, l_p), (bt_s, l_s) = final_shapes
        final_blocks_p = bt_p * l_p // tb
        nbp = final_blocks_p
        assert tb % bt_p == 0 and tb % bt_s == 0 and (tb // bt_p) % SUBLANES == 0 and (tb // bt_s) % SUBLANES == 0
        out_specs = [pl.BlockSpec((bt_p, tb // bt_p, d), lambda i, *_: (0, jnp.minimum(i, nbp - 1), 0)),
                     pl.BlockSpec((bt_s, tb // bt_s, d), lambda i, *_: (0, jnp.maximum(i - nbp, 0), 0))]
        out_shape = [jax.ShapeDtypeStruct((bt_p, l_p, d), F32), jax.ShapeDtypeStruct((bt_s, l_s, d), F32)]
        scratch.append(pltpu.VMEM((d // LANES, tb, LANES), F32))
    gs = pltpu.PrefetchScalarGridSpec(
        num_scalar_prefetch=4,
        grid=(nb,),
        in_specs=[pl.BlockSpec((tb, d), lambda i, *_: (i, 0)),
                  pl.BlockSpec((1, TOP_K, tb), lambda i, *_: (i, 0, 0)),
                  pl.BlockSpec((1, TOP_K, tb), lambda i, *_: (i, 0, 0)),
                  pl.BlockSpec((1, d), lambda i, *_: (0, 0)),
                  pl.BlockSpec(memory_space=pl.ANY)],
        out_specs=out_specs,
        scratch_shapes=scratch + [pltpu.SemaphoreType.DMA((3,))],
    )
    return pl.pallas_call(
        functools.partial(_combine_kernel, ne=ne, final_blocks_p=final_blocks_p),
        grid_spec=gs,
        out_shape=out_shape,
        compiler_params=_params(),
        name="moe_combine",
    )(pc8, loff, gst, used, x, lpos, gate, gfin, ys)


def _moe_layer(x, routed, g_ffn, w_up, b_up, w_dn, b_dn, g_final, *, layer, final_shapes):
    n, d = x.shape
    gate, lpos, cnt = routed
    ne = cnt.shape[1]
    tb, tm = TOKEN_BLOCK, FFN_TILE
    nb = n // tb
    cnt = cnt[:, :, 0]
    pc = (cnt + (SUBLANES - 1)) // SUBLANES * SUBLANES
    loff = jnp.cumsum(pc, axis=1) - pc
    rows_e = jnp.sum(pc, axis=0)
    tiles_e = (rows_e + tm - 1) // tm
    tile_end = jnp.cumsum(tiles_e)
    tile_start = tile_end - tiles_e
    gst = (tile_start * tm)[None, :] + jnp.cumsum(pc, axis=0) - pc
    part = tm // FFN_TAIL_PARTS
    tail = rows_e % tm
    short_e = jnp.where(tail > 0, (tm - tail) // part, 0)
    gap = tiles_e * tm - short_e * part - rows_e
    max_rows = n * TOP_K + nb * ne * (SUBLANES - 1)
    max_tiles = (max_rows + ne * (tm - SUBLANES)) // tm
    t = jnp.arange(max_tiles, dtype=jnp.int32)
    tile_short = jnp.sum(jnp.where((t[:, None] == (tile_end - 1)[None, :]) & (tiles_e > 0)[None, :],
                                   short_e[None, :], 0), axis=1)
    i32 = lambda a: a.astype(jnp.int32).reshape(-1)
    pc8, loff, gst = i32(pc // SUBLANES), i32(loff), i32(gst)
    used = i32(jnp.sum(pc, axis=1))
    xs = _sort_call(x, g_ffn, lpos, pc8, loff, gst, used, i32(gap // SUBLANES), i32(tile_start * tm + rows_e),
                    n_rows_out=max_tiles * tm, ne=ne)
    ys = _ffn_call(xs, w_up, b_up, w_dn, b_dn, i32(tile_start), i32(tiles_e), i32(tile_short), layer=layer)
    return _combine_call(x, lpos, gate, g_final, ys, pc8, loff, gst, used, ne=ne, final_shapes=final_shapes)


def _time_major(a):
    b, l, d = a.shape
    return a.transpose(1, 0, 2).reshape(l * b, d)


def _batch_major(a, b):
    n, d = a.shape
    return a.reshape(n // b, b, d).transpose(1, 0, 2)


def kernel(x_prompt, x_sample, cache_pool, state_ssm_re, state_ssm_im, norm_mix_g, norm_ffn_g, norm_final_g,
           pool_w, pool_scale, ssm_a_re, ssm_a_im, ssm_log_dt, ssm_b_re, ssm_b_im, ssm_c_re, ssm_c_im,
           ssm_d, ssm_glu_w, ssm_glu_gate, router_w, router_b, moe_w_up, moe_b_up, moe_w_down, moe_b_down):
    bp, lp, d = x_prompt.shape
    bs, ls, _ = x_sample.shape
    depth = norm_mix_g.shape[0]
    assert depth == 2 and cache_pool.shape[0] == 1 and state_ssm_re.shape[0] == 1
    n_p = bp * lp
    row = lambda v: v.reshape(1, -1)

    def route_w(i):
        wr_t = router_w[i].T
        wr_hi = wr_t.astype(BF16)
        wr_lo = (wr_t - wr_hi.astype(F32)).astype(BF16)
        return row(norm_ffn_g[i]), jnp.concatenate([wr_hi, wr_lo], axis=0), router_b[i][:, None]

    pw = pool_w[0].astype(BF16)
    hist_p = jnp.zeros((POOL_HIST * bp, d), F32)
    hist_s = _time_major(cache_pool[0])
    x1, pool_p, pool_s, *routed = _pool_call(x_prompt, x_sample, hist_p, hist_s, row(norm_mix_g[0]), pw,
                                             row(pool_scale[0]), route_w(0))
    x2 = _moe_layer(x1, routed, row(norm_ffn_g[0]), moe_w_up, moe_b_up, moe_w_down, moe_b_down,
                    row(norm_final_g), layer=0, final_shapes=None)

    ar, ai, wb, wc = _ssm_weights(ssm_a_re[0], ssm_a_im[0], ssm_log_dt[0], ssm_b_re[0], ssm_b_im[0],
                                  ssm_c_re[0], ssm_c_im[0])
    wglu = jnp.concatenate([ssm_glu_w[0], ssm_glu_gate[0]], axis=1).astype(BF16)
    g_groups, p_state = ssm_a_re.shape[1:]
    sc = g_groups * p_state
    zero_state = jnp.zeros((bp, sc), F32)
    x3, re_p, im_p, re_s, im_s, *routed = _ssm_call(
        x2, row(norm_mix_g[1]), ar, ai, wb, wc, row(ssm_d[0]), wglu, zero_state, zero_state,
        state_ssm_re[0].reshape(bs, sc), state_ssm_im[0].reshape(bs, sc), route_w(1), n_p=n_p)
    y_p, y_s = _moe_layer(x3, routed, row(norm_ffn_g[1]), moe_w_up, moe_b_up, moe_w_down, moe_b_down,
                          row(norm_final_g), layer=1, final_shapes=((bp, lp), (bs, ls)))

    st = lambda a, b: a.reshape(1, b, g_groups, p_state)
    return (y_p, y_s,
            _batch_major(pool_p, bp)[None], _batch_major(pool_s, bs)[None],
            st(re_p, bp), st(im_p, bp), st(re_s, bs), st(im_s, bs))
```
